```python
import jax, jax.numpy as jnp
from jax import lax
import numpy as np

D_MODEL = 2048
BATCH = 4
SEQ = 4096
DEPTH = 4
DEC_BATCH = 8
DEC_SEQ = 16
PAST_LEN = 2048

CHUNK = 64
N_MIXERS = 3
N_SGU_LAYERS = (DEPTH + 2) // 3
N_CONV_LAYERS = (DEPTH + 1) // 3
N_ATTN_LAYERS = DEPTH // 3
EPS = 1e-6
SGU_CHUNK = 128
SGU_DIM = 3 * D_MODEL
SGU_GROUPS = 8
SGU_GROUP_DIM = SGU_DIM // SGU_GROUPS
CONV_W = 31
WINDOW = 128
HEAD_DIM = 64
N_HEADS = D_MODEL // HEAD_DIM
KV_HEADS = N_HEADS // 8
Q_PER_KV = N_HEADS // KV_HEADS
Q_DIM = N_HEADS * HEAD_DIM
KV_DIM = KV_HEADS * HEAD_DIM
BAND_PREV = WINDOW // CHUNK
BAND = (BAND_PREV + 1) * CHUNK
NEG_INF = -1e30
PEER_HEADS = 8
N_KEYS = 128
N_EXPERTS = N_KEYS * N_KEYS
PEER_TOPK = 16
PEER_KEY_DIM = 256
PEER_HALF = PEER_KEY_DIM // 2
PEER_BLOCK = 128

kernel_name = "hybrid_streaming_encoder_step"


def rms_norm(x, g):
    xf = x.astype(jnp.float32)
    y = xf * lax.rsqrt(jnp.mean(xf * xf, axis=-1, keepdims=True) + EPS)
    return (y * g.astype(jnp.float32)).astype(x.dtype)


def layer_norm(x, g, b):
    xf = x.astype(jnp.float32)
    mu = jnp.mean(xf, axis=-1, keepdims=True)
    var = jnp.mean(jnp.square(xf - mu), axis=-1, keepdims=True)
    y = (xf - mu) * lax.rsqrt(var + EPS)
    return (y * g.astype(jnp.float32) + b.astype(jnp.float32)).astype(x.dtype)


def ada_params(c, w, b):
    m = jax.nn.silu(c) @ w + b
    return [t[:, None, :] for t in jnp.split(m, 6, axis=-1)]


def modulate(x, g, shift, scale):
    return rms_norm(x, g) * (1 + scale) + shift


def sgu_mask():
    t = jnp.arange(SGU_CHUNK)
    return (t[None, :] // CHUNK) <= (t[:, None] // CHUNK)


def sgu_mix(v, w_s, b_s):
    n_pos = v.shape[1]
    ws = (w_s * sgu_mask())[:, :n_pos, :n_pos].astype(v.dtype)
    return jnp.einsum('gts,nsge->ntge', ws, v) + b_s[:, :n_pos].T[None, :, :, None]


def chunk_mlp(h, w_in, b_in, ln_g, ln_b, w_s, b_s, w_out):
    b, s, _ = h.shape
    z = jax.nn.gelu(h @ w_in + b_in)
    u, v = z[..., :SGU_DIM], z[..., SGU_DIM:]
    v = layer_norm(v, ln_g, ln_b)
    n_pos = min(s, SGU_CHUNK)
    vc = v.reshape(b * (s // n_pos), n_pos, SGU_GROUPS, SGU_GROUP_DIM)
    mixed = sgu_mix(vc, w_s, b_s).reshape(b, s, SGU_DIM)
    return (u * mixed) @ w_out, v


def conv_module(h, hist, w_in, b_in, dw, dw_b, ln_g, ln_b, w_out, b_out):
    a = h @ w_in + b_in
    glu = a[..., :D_MODEL] * jax.nn.sigmoid(a[..., D_MODEL:])
    xp = jnp.concatenate([hist, glu], axis=1)
    y = lax.conv_general_dilated(xp, dw[:, None, :].astype(xp.dtype), window_strides=(1,), padding='VALID',
                                 dimension_numbers=('NWC', 'WIO', 'NWC'), feature_group_count=D_MODEL)
    y = jax.nn.silu(layer_norm(y + dw_b, ln_g, ln_b))
    return y @ w_out + b_out, xp[:, -(CONV_W - 1):]


def split_qkv(a):
    lead = a.shape[:-1]
    q = a[..., :Q_DIM].reshape(*lead, KV_HEADS, Q_PER_KV, HEAD_DIM)
    k = a[..., Q_DIM:Q_DIM + KV_DIM].reshape(*lead, KV_HEADS, HEAD_DIM)
    v = a[..., Q_DIM + KV_DIM:].reshape(*lead, KV_HEADS, HEAD_DIM)
    return q, k, v


def band_attention(q, k, v, key_valid, sinks):
    n_q, n_k = q.shape[-4], k.shape[-3]
    s = jnp.einsum('...qgrd,...kgd->...grqk', q, k).astype(jnp.float32) * (HEAD_DIM ** -0.5)
    slopes = jnp.exp2(-8.0 * jnp.arange(1, N_HEADS + 1, dtype=jnp.float32) / N_HEADS).reshape(KV_HEADS, Q_PER_KV)
    dist = jnp.abs(WINDOW + jnp.arange(n_q)[:, None] - jnp.arange(n_k)[None, :]).astype(jnp.float32)
    s = s - slopes[:, :, None, None] * dist
    s = jnp.where(key_valid[..., None, None, None, :], s, NEG_INF)
    sink = sinks.astype(jnp.float32).reshape(KV_HEADS, Q_PER_KV)[:, :, None, None]
    m = jnp.maximum(jnp.max(s, axis=-1, keepdims=True), sink)
    p = jnp.exp(s - m)
    p = p / (jnp.sum(p, axis=-1, keepdims=True) + jnp.exp(sink - m))
    return jnp.einsum('...grqk,...kgd->...qgrd', p.astype(v.dtype), v)


def swa_prompt(h, w_qkv, b_qkv, sinks, w_o):
    b, s, _ = h.shape
    n_c = s // CHUNK
    q, k, v = split_qkv(h @ w_qkv + b_qkv)
    qc = q.reshape(b, n_c, CHUNK, KV_HEADS, Q_PER_KV, HEAD_DIM)
    pad = ((0, 0), (WINDOW, 0), (0, 0), (0, 0))
    kp = jnp.pad(k, pad).reshape(b, n_c + BAND_PREV, CHUNK, KV_HEADS, HEAD_DIM)
    vp = jnp.pad(v, pad).reshape(b, n_c + BAND_PREV, CHUNK, KV_HEADS, HEAD_DIM)
    kb = jnp.concatenate([kp[:, j:j + n_c] for j in range(BAND_PREV + 1)], axis=2)
    vb = jnp.concatenate([vp[:, j:j + n_c] for j in range(BAND_PREV + 1)], axis=2)
    k_pos = jnp.arange(n_c)[:, None] * CHUNK - WINDOW + jnp.arange(BAND)[None, :]
    o = band_attention(qc, kb, vb, (k_pos >= 0)[None], sinks).reshape(b, s, Q_DIM)
    return o @ w_o, k[:, -WINDOW:], v[:, -WINDOW:]


def swa_sample(h, cache_k, cache_v, w_qkv, b_qkv, sinks, w_o):
    b, t, _ = h.shape
    q, k, v = split_qkv(h @ w_qkv + b_qkv)
    kk = jnp.concatenate([cache_k.astype(k.dtype), k], axis=1)
    vv = jnp.concatenate([cache_v.astype(v.dtype), v], axis=1)
    valid = jnp.ones((1, kk.shape[1]), dtype=bool)
    o = band_attention(q, kk, vv, valid, sinks).reshape(b, t, Q_DIM)
    return o @ w_o, kk[:, -WINDOW:], vv[:, -WINDOW:]


def peer(x, w_q, subkeys, u_tab, v_tab):
    lead = x.shape[:-1]
    xf = x.reshape(-1, D_MODEL)
    n = xf.shape[0]
    n_blocks = -(-n // PEER_BLOCK)
    xb = jnp.pad(xf, ((0, n_blocks * PEER_BLOCK - n), (0, 0))).reshape(n_blocks, PEER_BLOCK, D_MODEL)

    def block(xt):
        q = (xt @ w_q).reshape(PEER_BLOCK, PEER_HEADS, 2, PEER_HALF)
        s = jnp.einsum('thcd,hckd->thck', q, subkeys).astype(jnp.float32)
        sv, si = lax.top_k(s, PEER_TOPK)
        cand = (sv[:, :, 0, :, None] + sv[:, :, 1, None, :]).reshape(PEER_BLOCK, PEER_HEADS, PEER_TOPK * PEER_TOPK)
        best, bi = lax.top_k(cand, PEER_TOPK)
        i1 = jnp.take_along_axis(si[:, :, 0], bi // PEER_TOPK, axis=-1)
        i2 = jnp.take_along_axis(si[:, :, 1], bi % PEER_TOPK, axis=-1)
        e = i1 * N_KEYS + i2
        g = jax.nn.softmax(best, axis=-1)
        a = jnp.einsum('thkd,td->thk', u_tab[e], xt)
        w = (g * jax.nn.gelu(a.astype(jnp.float32))).astype(xt.dtype)
        return jnp.einsum('thk,thkd->td', w, v_tab[e])

    out = lax.map(block, xb).reshape(n_blocks * PEER_BLOCK, D_MODEL)[:n]
    return out.reshape(*lead, D_MODEL)


def setup_inputs(seed: int = 0) -> dict:
    key = jax.random.key(seed)
    ks = iter(jax.random.split(key, 48))

    def nrm(shape, std):
        return std * jax.random.normal(next(ks), shape, jnp.float32)

    def gain(shape):
        return 1.0 + nrm(shape, 0.05)

    d = D_MODEL
    return {
        "x_prompt": nrm((BATCH, SEQ, d), 1.0),
        "x_sample": nrm((DEC_BATCH, DEC_SEQ, d), 1.0),
        "cache_k_win": nrm((N_ATTN_LAYERS, DEC_BATCH, WINDOW, KV_HEADS, HEAD_DIM), 1.0),
        "cache_v_win": nrm((N_ATTN_LAYERS, DEC_BATCH, WINDOW, KV_HEADS, HEAD_DIM), 1.0),
        "state_conv": nrm((N_CONV_LAYERS, DEC_BATCH, CONV_W - 1, d), 0.5),
        "c_prompt": nrm((BATCH, d), 1.0),
        "c_sample": nrm((DEC_BATCH, d), 1.0),
        "norm_mix_g": gain((DEPTH, d)),
        "norm_ch_g": gain((DEPTH, d)),
        "norm_final_g": gain((d,)),
        "ada_w": nrm((DEPTH, d, 6 * d), 0.5 * d ** -0.5),
        "ada_b": nrm((DEPTH, 6 * d), 0.02),
        "sgu_w_in": nrm((N_SGU_LAYERS, d, 2 * SGU_DIM), d ** -0.5),
        "sgu_b_in": nrm((N_SGU_LAYERS, 2 * SGU_DIM), 0.02),
        "sgu_ln_g": gain((N_SGU_LAYERS, SGU_DIM)),
        "sgu_ln_b": nrm((N_SGU_LAYERS, SGU_DIM), 0.02),
        "sgu_w_s": nrm((N_SGU_LAYERS, SGU_GROUPS, SGU_CHUNK, SGU_CHUNK), SGU_CHUNK ** -0.5),
        "sgu_b_s": 1.0 + nrm((N_SGU_LAYERS, SGU_GROUPS, SGU_CHUNK), 0.1),
        "sgu_w_out": nrm((N_SGU_LAYERS, SGU_DIM, d), SGU_DIM ** -0.5),
        "conv_w_in": nrm((N_CONV_LAYERS, d, 2 * d), d ** -0.5),
        "conv_b_in": nrm((N_CONV_LAYERS, 2 * d), 0.02),
        "conv_dw": nrm((N_CONV_LAYERS, CONV_W, d), CONV_W ** -0.5),
        "conv_dw_b": nrm((N_CONV_LAYERS, d), 0.02),
        "conv_ln_g": gain((N_CONV_LAYERS, d)),
        "conv_ln_b": nrm((N_CONV_LAYERS, d), 0.02),
        "conv_w_out": nrm((N_CONV_LAYERS, d, d), d ** -0.5),
        "conv_b_out": nrm((N_CONV_LAYERS, d), 0.02),
        "attn_w_qkv": nrm((N_ATTN_LAYERS, d, Q_DIM + 2 * KV_DIM), d ** -0.5),
        "attn_b_qkv": nrm((N_ATTN_LAYERS, Q_DIM + 2 * KV_DIM), 0.02),
        "attn_sinks": nrm((N_ATTN_LAYERS, N_HEADS), 0.5),
        "attn_w_o": nrm((N_ATTN_LAYERS, Q_DIM, d), Q_DIM ** -0.5),
        "peer_w_q": nrm((DEPTH, d, PEER_HEADS * PEER_KEY_DIM), d ** -0.5),
        "peer_subkeys": nrm((DEPTH, PEER_HEADS, 2, N_KEYS, PEER_HALF), PEER_HALF ** -0.5),
        "peer_u": nrm((DEPTH, N_EXPERTS, d), d ** -0.5),
        "peer_v": nrm((DEPTH, N_EXPERTS, d), PEER_HEADS ** -0.5),
    }


def reference(x_prompt, x_sample, cache_k_win, cache_v_win, state_conv, c_prompt, c_sample,
              norm_mix_g, norm_ch_g, norm_final_g, ada_w, ada_b,
              sgu_w_in, sgu_b_in, sgu_ln_g, sgu_ln_b, sgu_w_s, sgu_b_s, sgu_w_out,
              conv_w_in, conv_b_in, conv_dw, conv_dw_b, conv_ln_g, conv_ln_b, conv_w_out, conv_b_out,
              attn_w_qkv, attn_b_qkv, attn_sinks, attn_w_o,
              peer_w_q, peer_subkeys, peer_u, peer_v):
    xp, xs = x_prompt, x_sample
    ia = ib = ic = 0
    conv_p, conv_s, kwin_p, vwin_p, kwin_s, vwin_s, sgu_s = [], [], [], [], [], [], []
    for layer in range(DEPTH):
        sh1p, sc1p, g1p, sh2p, sc2p, g2p = ada_params(c_prompt, ada_w[layer], ada_b[layer])
        sh1s, sc1s, g1s, sh2s, sc2s, g2s = ada_params(c_sample, ada_w[layer], ada_b[layer])
        hp = modulate(xp, norm_mix_g[layer], sh1p, sc1p)
        hs = modulate(xs, norm_mix_g[layer], sh1s, sc1s)
        kind = layer % N_MIXERS
        if kind == 0:
            args = (sgu_w_in[ia], sgu_b_in[ia], sgu_ln_g[ia], sgu_ln_b[ia], sgu_w_s[ia], sgu_b_s[ia], sgu_w_out[ia])
            op, _ = chunk_mlp(hp, *args)
            os_, v_rows = chunk_mlp(hs, *args)
            sgu_s.append(v_rows)
            ia += 1
        elif kind == 1:
            args = (conv_w_in[ib], conv_b_in[ib], conv_dw[ib], conv_dw_b[ib], conv_ln_g[ib], conv_ln_b[ib],
                    conv_w_out[ib], conv_b_out[ib])
            op, st_p = conv_module(hp, jnp.zeros((hp.shape[0], CONV_W - 1, D_MODEL), hp.dtype), *args)
            os_, st_s = conv_module(hs, state_conv[ib].astype(hs.dtype), *args)
            conv_p.append(st_p)
            conv_s.append(st_s)
            ib += 1
        else:
            args = (attn_w_qkv[ic], attn_b_qkv[ic], attn_sinks[ic], attn_w_o[ic])
            op, kp, vp = swa_prompt(hp, *args)
            os_, ks, vs = swa_sample(hs, cache_k_win[ic], cache_v_win[ic], *args)
            kwin_p.append(kp)
            vwin_p.append(vp)
            kwin_s.append(ks)
            vwin_s.append(vs)
            ic += 1
        xp = xp + g1p * op
        xs = xs + g1s * os_
        hp = modulate(xp, norm_ch_g[layer], sh2p, sc2p)
        hs = modulate(xs, norm_ch_g[layer], sh2s, sc2s)
        peer_args = (peer_w_q[layer], peer_subkeys[layer], peer_u[layer], peer_v[layer])
        xp = xp + g2p * peer(hp, *peer_args)
        xs = xs + g2s * peer(hs, *peer_args)
    y_prompt = rms_norm(xp, norm_final_g)
    y_sample = rms_norm(xs, norm_final_g)
    return (y_prompt, y_sample, jnp.stack(conv_p), jnp.stack(kwin_p), jnp.stack(vwin_p),
            jnp.stack(conv_s), jnp.stack(kwin_s), jnp.stack(vwin_s), jnp.stack(sgu_s))
```

```python
import functools

import jax
import jax.numpy as jnp
from jax import lax
from jax.experimental import pallas as pl
from jax.experimental.pallas import tpu as pltpu

F32 = jnp.float32
BF16 = jnp.bfloat16

EPS = 1e-6
CHUNK = 64
SGU_CHUNK = 128
SGU_GROUPS = 8
PEER_TOPK = 16
NEG_INF = -1e30
N_MIXERS = 3

V7X_VMEM_BYTES = 64 * 1024 * 1024
VMEM_LIMIT = V7X_VMEM_BYTES - 8 * 1024 * 1024
LANE = 128


def _cp(*sem):
    return pltpu.CompilerParams(dimension_semantics=sem, vmem_limit_bytes=VMEM_LIMIT)


def _tile(n, pref, mult=LANE):
    if n <= pref:
        return n
    t = (pref // mult) * mult
    while t >= mult:
        if n % t == 0:
            return t
        t -= mult
    return n


def _gelu(x):
    return jax.nn.gelu(x, approximate=True)


def _ada_kernel(c_ref, w_ref, b_ref, o_ref):
    c = c_ref[...]
    a = (c * jax.nn.sigmoid(c)).astype(BF16)
    o_ref[0] = jnp.dot(a, w_ref[0].astype(BF16), preferred_element_type=F32) + b_ref[0]


def ada_all(c_all, ada_w, ada_b):
    n_layers, d, n = ada_w.shape
    rows = c_all.shape[0]
    tn = _tile(n, 1024)
    return pl.pallas_call(
        _ada_kernel,
        grid=(n_layers, n // tn),
        in_specs=[pl.BlockSpec((rows, d), lambda l, j: (0, 0)),
                  pl.BlockSpec((1, d, tn), lambda l, j: (l, 0, j)),
                  pl.BlockSpec((1, 1, tn), lambda l, j: (l, 0, j))],
        out_specs=pl.BlockSpec((1, rows, tn), lambda l, j: (l, 0, j)),
        out_shape=jax.ShapeDtypeStruct((n_layers, rows, n), F32),
        compiler_params=_cp("arbitrary", "arbitrary"),
        name="ada",
    )(c_all, ada_w, ada_b.reshape(n_layers, 1, n))


def _mod_spec(arr, tm, seq_len, tn=None, col_of=None):
    _, r, d = arr.shape
    if tn is None:
        return pl.BlockSpec((1, r, d), lambda i, *_: ((i * tm) // seq_len, 0, 0))
    return pl.BlockSpec((1, r, tn), lambda i, j: ((i * tm) // seq_len, 0, col_of(j)))


def _norm_kernel(*refs, modulated, transposed):
    if modulated:
        x_ref, g_ref, sc_ref, sh_ref, o_ref = refs
    else:
        x_ref, g_ref, o_ref = refs
    x = x_ref[...]
    y = x * lax.rsqrt(jnp.mean(x * x, axis=-1, keepdims=True) + EPS) * g_ref[...]
    if modulated:
        y = y * (1.0 + sc_ref[0]) + sh_ref[0]
    if transposed:
        o_ref[...] = y.T.astype(o_ref.dtype)
    else:
        o_ref[...] = y.astype(o_ref.dtype)


def norm(x, g, scale=None, shift=None, *, seq_len, transposed=False, out_dtype=BF16):
    t, d = x.shape
    tm = _tile(seq_len, 512)
    modulated = scale is not None
    in_specs = [pl.BlockSpec((tm, d), lambda i: (i, 0)), pl.BlockSpec((1, d), lambda i: (0, 0))]
    args = [x, g.reshape(1, d)]
    if modulated:
        in_specs += [_mod_spec(scale, tm, seq_len), _mod_spec(shift, tm, seq_len)]
        args += [scale, shift]
    if transposed:
        out_spec = pl.BlockSpec((d, tm), lambda i: (0, i))
        out_shape = jax.ShapeDtypeStruct((d, t), out_dtype)
    else:
        out_spec = pl.BlockSpec((tm, d), lambda i: (i, 0))
        out_shape = jax.ShapeDtypeStruct((t, d), out_dtype)
    return pl.pallas_call(
        functools.partial(_norm_kernel, modulated=modulated, transposed=transposed),
        grid=(t // tm,), in_specs=in_specs, out_specs=out_spec, out_shape=out_shape,
        compiler_params=_cp("arbitrary"), name="norm",
    )(*args)


def _mm_kernel(*refs, n_w, has_bias, epilogue):
    h_ref = refs[0]
    w_refs = refs[1:1 + n_w]
    pos = 1 + n_w
    b_refs = refs[pos:pos + n_w] if has_bias else ()
    pos += n_w if has_bias else 0
    h = h_ref[...]
    accs = []
    for k in range(n_w):
        acc = jnp.dot(h, w_refs[k][...], preferred_element_type=F32)
        if has_bias:
            acc = acc + b_refs[k][...]
        accs.append(acc)
    rest = refs[pos:]
    if epilogue == "sgu_in":
        rest[0][...] = _gelu(accs[0]).astype(BF16)
        rest[1][...] = _gelu(accs[1])
    elif epilogue == "glu":
        rest[0][...] = accs[0] * jax.nn.sigmoid(accs[1])
    elif epilogue == "plain":
        rest[0][...] = accs[0]
    elif epilogue == "residual":
        x_ref, gate_ref, o_ref = rest
        o_ref[...] = x_ref[...] + gate_ref[0] * accs[0]
    else:
        raise ValueError(epilogue)


def matmul(h, w, bias, epilogue, *, col_offsets=(0,), n_out, out_dtypes, tm_pref, tn_pref,
           x=None, gate=None, seq_len=None):
    t, k_dim = h.shape
    tm = _tile(t if seq_len is None else seq_len, tm_pref, 8)
    tn = _tile(n_out, tn_pref)
    n_w = len(col_offsets)
    has_bias = bias is not None
    in_specs = [pl.BlockSpec((tm, k_dim), lambda i, j: (i, 0))]
    args = [h]
    for off in col_offsets:
        assert off % tn == 0
        in_specs.append(pl.BlockSpec((k_dim, tn), lambda i, j, o=off // tn: (0, j + o)))
        args.append(w)
    if has_bias:
        b2 = bias.reshape(1, -1)
        for off in col_offsets:
            in_specs.append(pl.BlockSpec((1, tn), lambda i, j, o=off // tn: (0, j + o)))
            args.append(b2)
    if epilogue == "residual":
        in_specs += [pl.BlockSpec((tm, tn), lambda i, j: (i, j)),
                     _mod_spec(gate, tm, seq_len, tn, lambda j: j)]
        args += [x, gate]
    out_specs = [pl.BlockSpec((tm, tn), lambda i, j: (i, j)) for _ in out_dtypes]
    out_shape = [jax.ShapeDtypeStruct((t, n_out), dt) for dt in out_dtypes]
    res = pl.pallas_call(
        functools.partial(_mm_kernel, n_w=n_w, has_bias=has_bias, epilogue=epilogue),
        grid=(t // tm, n_out // tn), in_specs=in_specs, out_specs=out_specs, out_shape=out_shape,
        compiler_params=_cp("arbitrary", "arbitrary"), name="mm_" + epilogue,
    )(*args)
    return res if len(res) > 1 else res[0]


def _sgu_mix_kernel(v_ref, u_ref, lg_ref, lb_ref, ws_ref, mask_ref, bst_ref, o_ref, *vout,
                    groups):
    v = v_ref[...]
    mu = jnp.mean(v, axis=-1, keepdims=True)
    vc = v - mu
    var = jnp.mean(vc * vc, axis=-1, keepdims=True)
    vn = vc * lax.rsqrt(var + EPS) * lg_ref[...] + lb_ref[...]
    if vout:
        vout[0][...] = vn
    eg = v.shape[1] // groups
    for g in range(groups):
        w = (ws_ref[g] * mask_ref[...]).astype(BF16)
        vg = vn[:, g * eg:(g + 1) * eg].astype(BF16)
        mixed = jnp.dot(w, vg, preferred_element_type=F32) + bst_ref[:, g:g + 1]
        u = u_ref[:, g * eg:(g + 1) * eg].astype(F32)
        o_ref[:, g * eg:(g + 1) * eg] = (u * mixed).astype(BF16)


def sgu_mix(v, u, ln_g, ln_b, ws, mask, bst, *, emit_v):
    t, e = v.shape
    n = ws.shape[1]
    groups = ws.shape[0]
    row = lambda i: (i, 0)
    full2 = lambda i: (0, 0)
    out_specs = [pl.BlockSpec((n, e), row)]
    out_shape = [jax.ShapeDtypeStruct((t, e), BF16)]
    if emit_v:
        out_specs.append(pl.BlockSpec((n, e), row))
        out_shape.append(jax.ShapeDtypeStruct((t, e), F32))
    return pl.pallas_call(
        functools.partial(_sgu_mix_kernel, groups=groups),
        grid=(t // n,),
        in_specs=[pl.BlockSpec((n, e), row), pl.BlockSpec((n, e), row),
                  pl.BlockSpec((1, e), full2), pl.BlockSpec((1, e), full2),
                  pl.BlockSpec((groups, n, n), lambda i: (0, 0, 0)),
                  pl.BlockSpec((n, n), full2), pl.BlockSpec((n, groups), full2)],
        out_specs=out_specs, out_shape=out_shape,
        compiler_params=_cp("arbitrary"), name="sgu_mix",
    )(v, u, ln_g.reshape(1, e), ln_b.reshape(1, e), ws, mask, bst)


HALO = 32
CONV_COLS = 256


def _conv_kernel(*refs, tt, width, single):
    if single:
        cur_ref, hist_ref, dw_ref, dwb_ref, lg_ref, lb_ref, o_ref, xp_ref, y_ref = refs
    else:
        cur_ref, prev_ref, hist_ref, dw_ref, dwb_ref, lg_ref, lb_ref, o_ref, xp_ref, y_ref = refs
    d = cur_ref.shape[1]
    xp_ref[HALO:HALO + tt, :] = cur_ref[...]
    if single:
        xp_ref[0:HALO, :] = hist_ref[0]
    else:
        first = pl.program_id(1) == 0

        @pl.when(first)
        def _():
            xp_ref[0:HALO, :] = hist_ref[0]

        @pl.when(jnp.logical_not(first))
        def _():
            xp_ref[0:HALO, :] = prev_ref[...]

    base = HALO - (width - 1)
    cw = min(CONV_COLS, d)
    for c in range(d // cw):
        cols = slice(c * cw, (c + 1) * cw)
        acc = jnp.zeros((tt, cw), F32)
        for j in range(width):
            acc = acc + dw_ref[j:j + 1, cols] * xp_ref[base + j:base + j + tt, cols]
        y_ref[:, cols] = acc + dwb_ref[:, cols]
    y = y_ref[...]
    mu = jnp.mean(y, axis=-1, keepdims=True)
    yc = y - mu
    var = jnp.mean(yc * yc, axis=-1, keepdims=True)
    yn = yc * lax.rsqrt(var + EPS) * lg_ref[...] + lb_ref[...]
    o_ref[...] = (yn * jax.nn.sigmoid(yn)).astype(BF16)


def conv_core(glu, hist, dw, dw_b, ln_g, ln_b, *, seq_len):
    t, d = glu.shape
    b = t // seq_len
    width = dw.shape[0]
    tt = _tile(seq_len, 128, HALO)
    single = tt == seq_len
    nb = seq_len // tt
    vec = lambda a: a.reshape(1, d)
    full2 = lambda bi, i: (0, 0)
    in_specs = [pl.BlockSpec((tt, d), lambda bi, i: (bi * nb + i, 0))]
    args = [glu]
    if not single:
        per = tt // HALO
        in_specs.append(pl.BlockSpec((HALO, d), lambda bi, i: (jnp.maximum((bi * nb + i) * per - 1, 0), 0)))
        args.append(glu)
    in_specs += [pl.BlockSpec((1, HALO, d), lambda bi, i: (bi, 0, 0)),
                 pl.BlockSpec((width, d), full2),
                 pl.BlockSpec((1, d), full2), pl.BlockSpec((1, d), full2), pl.BlockSpec((1, d), full2)]
    args += [hist, dw, vec(dw_b), vec(ln_g), vec(ln_b)]
    return pl.pallas_call(
        functools.partial(_conv_kernel, tt=tt, width=width, single=single),
        grid=(b, nb), in_specs=in_specs,
        out_specs=pl.BlockSpec((tt, d), lambda bi, i: (bi * nb + i, 0)),
        out_shape=jax.ShapeDtypeStruct((t, d), BF16),
        scratch_shapes=[pltpu.VMEM((HALO + tt, d), F32), pltpu.VMEM((tt, d), F32)],
        compiler_params=_cp("arbitrary", "arbitrary"), name="conv_core",
    )(*args)


def _attn_kernel(sink_ref, q_ref, k_ref, v_ref, o_ref, *, kv_heads, q_per_kv, head_dim, window,
                 n_heads, mask_front):
    nq = q_ref.shape[0]
    nk = k_ref.shape[2]
    qi = lax.broadcasted_iota(jnp.int32, (nq, nk), 0)
    kj = lax.broadcasted_iota(jnp.int32, (nq, nk), 1)
    dist = jnp.abs(window + qi - kj).astype(F32)
    if mask_front:
        valid = kj >= window - pl.program_id(1) * nq
    scale = head_dim ** -0.5
    for g in range(kv_heads):
        k = k_ref[0, 0, :, g * head_dim:(g + 1) * head_dim].astype(BF16)
        v = v_ref[0, 0, :, g * head_dim:(g + 1) * head_dim].astype(BF16)
        for r in range(q_per_kv):
            hd = g * q_per_kv + r
            cols = slice(hd * head_dim, (hd + 1) * head_dim)
            q = q_ref[:, cols].astype(BF16)
            s = lax.dot_general(q, k, (((1,), (1,)), ((), ())), preferred_element_type=F32) * scale
            slope = 2.0 ** (-8.0 * (hd + 1) / n_heads)
            s = s - slope * dist
            if mask_front:
                s = jnp.where(valid, s, NEG_INF)
            sink = sink_ref[hd]
            m = jnp.maximum(jnp.max(s, axis=-1, keepdims=True), sink)
            p = jnp.exp(s - m)
            den = jnp.sum(p, axis=-1, keepdims=True) + jnp.exp(sink - m)
            p = (p / den).astype(BF16)
            o_ref[:, cols] = jnp.dot(p, v, preferred_element_type=F32).astype(BF16)


def attention(qkv, kb, vb, sinks, *, q_dim, head_dim, window, mask_front):
    t = qkv.shape[0]
    b, nc, band, kv_dim = kb.shape
    nq = t // (b * nc)
    n_heads = q_dim // head_dim
    kv_heads = kv_dim // head_dim
    kern = functools.partial(_attn_kernel, kv_heads=kv_heads, q_per_kv=n_heads // kv_heads,
                             head_dim=head_dim, window=window, n_heads=n_heads, mask_front=mask_front)
    return pl.pallas_call(
        kern, grid=(b, nc),
        in_specs=[pl.BlockSpec(memory_space=pltpu.SMEM),
                  pl.BlockSpec((nq, q_dim), lambda bi, c: (bi * nc + c, 0)),
                  pl.BlockSpec((1, 1, band, kv_dim), lambda bi, c: (bi, c, 0, 0)),
                  pl.BlockSpec((1, 1, band, kv_dim), lambda bi, c: (bi, c, 0, 0))],
        out_specs=pl.BlockSpec((nq, q_dim), lambda bi, c: (bi * nc + c, 0)),
        out_shape=jax.ShapeDtypeStruct((t, q_dim), BF16),
        compiler_params=_cp("arbitrary", "arbitrary"), name="attention",
    )(sinks, qkv, kb, vb)


def _peer_score_kernel(ht_ref, wqt_ref, sk_ref, o_ref):
    qt = jnp.dot(wqt_ref[...], ht_ref[...], preferred_element_type=F32)
    o_ref[...] = jnp.dot(sk_ref[0], qt.astype(BF16), preferred_element_type=F32)


def peer_scores(ht, wqt, skbd):
    d, t = ht.shape
    heads, hk, _ = skbd.shape
    tt = _tile(t, 512)
    return pl.pallas_call(
        _peer_score_kernel, grid=(t // tt, heads),
        in_specs=[pl.BlockSpec((d, tt), lambda i, h: (0, i)),
                  pl.BlockSpec((hk, d), lambda i, h: (h, 0)),
                  pl.BlockSpec((1, hk, hk), lambda i, h: (h, 0, 0))],
        out_specs=pl.BlockSpec((hk, tt), lambda i, h: (h, i)),
        out_shape=jax.ShapeDtypeStruct((heads * hk, t), F32),
        compiler_params=_cp("arbitrary", "arbitrary"), name="peer_scores",
    )(ht, wqt, skbd)


def _peer_stats_kernel(s_ref, cnt_ref, e1_ref, r2_ref, e2_ref, a_scr, b_scr, c_scr, *, topk, n_cand):
    nk = s_ref.shape[0] // 2
    s1 = s_ref[0:nk, :]
    s2 = s_ref[nk:2 * nk, :]

    def top_values(s, scr):
        w = s
        for r in range(topk):
            m = jnp.max(w, axis=0, keepdims=True)
            scr[r:r + 1, :] = m
            w = jnp.where(w == m, NEG_INF, w)

    top_values(s1, a_scr)
    top_values(s2, b_scr)
    k = 0
    for r in range(topk):
        for j in range(topk // (r + 1)):
            c_scr[k:k + 1, :] = a_scr[r:r + 1, :] + b_scr[j:j + 1, :]
            k += 1
    rows = c_scr.shape[0]
    c_scr[n_cand:rows, :] = jnp.full((rows - n_cand, c_scr.shape[1]), NEG_INF, F32)
    cand = c_scr[...]
    ridx = lax.broadcasted_iota(jnp.int32, cand.shape, 0)
    m0 = jnp.max(cand, axis=0, keepdims=True)
    z = jnp.zeros_like(m0)
    m = m0
    for r in range(topk):
        m = jnp.max(cand, axis=0, keepdims=True)
        z = z + jnp.exp(m - m0)
        first = jnp.min(jnp.where(cand == m, ridx, rows), axis=0, keepdims=True)
        cand = jnp.where(ridx == first, NEG_INF, cand)
    tau = m
    cnt = jnp.zeros_like(s1)
    rank2 = jnp.ones_like(s2)
    for j in range(topk):
        bj = b_scr[j:j + 1, :]
        cnt = cnt + jnp.where(s1 + bj >= tau, 1.0, 0.0)
        rank2 = rank2 + jnp.where(bj > s2, 1.0, 0.0)
    cnt_ref[0] = cnt
    r2_ref[0] = rank2
    e1_ref[0] = jnp.exp(s1 - a_scr[0:1, :])
    e2_ref[0] = jnp.exp(s2 - b_scr[0:1, :]) / z


def peer_stats(st, heads):
    hk2, t = st.shape
    hk = hk2 // heads
    nk = hk // 2
    tt = LANE
    n_cand = sum(PEER_TOPK // (r + 1) for r in range(PEER_TOPK))
    cand_rows = -(-n_cand // 8) * 8
    out = jax.ShapeDtypeStruct((heads, nk, t), F32)
    spec = pl.BlockSpec((1, nk, tt), lambda i, h: (h, 0, i))
    return pl.pallas_call(
        functools.partial(_peer_stats_kernel, topk=PEER_TOPK, n_cand=n_cand),
        grid=(t // tt, heads),
        in_specs=[pl.BlockSpec((hk, tt), lambda i, h: (h, i))],
        out_specs=[spec, spec, spec, spec], out_shape=[out, out, out, out],
        scratch_shapes=[pltpu.VMEM((PEER_TOPK, tt), F32), pltpu.VMEM((PEER_TOPK, tt), F32),
                        pltpu.VMEM((cand_rows, tt), F32)],
        compiler_params=_cp("arbitrary", "arbitrary"), name="peer_stats",
    )(st)


def _peer_dense_kernel(ht_ref, u_ref, vt_ref, cnt_ref, e1_ref, r2_ref, e2_ref, o_ref, a_scr, w_scr,
                       *, heads, n_i1, nk, tb):
    a_scr[...] = jnp.dot(u_ref[...], ht_ref[...], preferred_element_type=F32)

    def tok_chunk(tc, carry):
        lanes = pl.ds(pl.multiple_of(tc * LANE, LANE), LANE)
        for il in range(n_i1):
            rows = slice(il * nk, (il + 1) * nk)
            gate = jnp.zeros((nk, LANE), F32)
            for h in range(heads):
                cnt = cnt_ref[h, il:il + 1, lanes]
                e1 = e1_ref[h, il:il + 1, lanes]
                gate = gate + jnp.where(r2_ref[h, :, lanes] <= cnt, e2_ref[h, :, lanes], 0.0) * e1
            w_scr[rows, lanes] = (_gelu(a_scr[rows, lanes]) * gate).astype(BF16)
        return carry

    lax.fori_loop(0, tb // LANE, tok_chunk, 0)
    contrib = jnp.dot(vt_ref[...], w_scr[...], preferred_element_type=F32)
    j = pl.program_id(1)

    @pl.when(j == 0)
    def _():
        o_ref[...] = contrib

    @pl.when(j > 0)
    def _():
        o_ref[...] += contrib


def peer_dense(ht, u_bf, vt_bf, cnt, e1, r2, e2):
    d, t = ht.shape
    n_exp = u_bf.shape[0]
    heads, nk, _ = cnt.shape
    tb = _tile(t, 512)
    n_i1 = 8
    ec = n_i1 * nk
    tok3 = lambda i, j: (0, 0, i)
    return pl.pallas_call(
        functools.partial(_peer_dense_kernel, heads=heads, n_i1=n_i1, nk=nk, tb=tb),
        grid=(t // tb, n_exp // ec),
        in_specs=[pl.BlockSpec((d, tb), lambda i, j: (0, i)),
                  pl.BlockSpec((ec, d), lambda i, j: (j, 0)),
                  pl.BlockSpec((d, ec), lambda i, j: (0, j)),
                  pl.BlockSpec((heads, n_i1, tb), lambda i, j: (0, j, i)),
                  pl.BlockSpec((heads, n_i1, tb), lambda i, j: (0, j, i)),
                  pl.BlockSpec((heads, nk, tb), tok3),
                  pl.BlockSpec((heads, nk, tb), tok3)],
        out_specs=pl.BlockSpec((d, tb), lambda i, j: (0, i)),
        out_shape=jax.ShapeDtypeStruct((d, t), F32),
        scratch_shapes=[pltpu.VMEM((ec, tb), F32), pltpu.VMEM((ec, tb), BF16)],
        compiler_params=_cp("arbitrary", "arbitrary"), name="peer_dense",
    )(ht, u_bf, vt_bf, cnt, e1, r2, e2)


def _resid_t_kernel(x_ref, yt_ref, gate_ref, o_ref):
    o_ref[...] = x_ref[...] + gate_ref[0] * yt_ref[...].T


def residual_t(x, yt, gate, *, seq_len):
    t, d = x.shape
    tm = _tile(seq_len, 512)
    return pl.pallas_call(
        _resid_t_kernel, grid=(t // tm,),
        in_specs=[pl.BlockSpec((tm, d), lambda i: (i, 0)), pl.BlockSpec((d, tm), lambda i: (0, i)),
                  _mod_spec(gate, tm, seq_len)],
        out_specs=pl.BlockSpec((tm, d), lambda i: (i, 0)),
        out_shape=jax.ShapeDtypeStruct((t, d), F32),
        compiler_params=_cp("arbitrary"), name="residual_t",
    )(x, yt, gate)


def _peer_layer(x, g, sc, sh, gate, pw, *, seq_len):
    ht = norm(x, g, sc, sh, seq_len=seq_len, transposed=True)
    st = peer_scores(ht, pw["wqt"], pw["skbd"])
    cnt, e1, r2, e2 = peer_stats(st, pw["skbd"].shape[0])
    yt = peer_dense(ht, pw["u"], pw["vt"], cnt, e1, r2, e2)
    return residual_t(x, yt, gate, seq_len=seq_len)


def _sgu_layer(x, h, gate, w, *, seq_len, emit_v):
    e = w["w_out"].shape[0]
    u, v = matmul(h, w["w_in"], w["b_in"], "sgu_in", col_offsets=(0, e), n_out=e,
                  out_dtypes=(BF16, F32), tm_pref=1024, tn_pref=512)
    res = sgu_mix(v, u, w["ln_g"], w["ln_b"], w["ws"], w["mask"], w["bst"], emit_v=emit_v)
    gated = res[0]
    xn = matmul(gated, w["w_out"], None, "residual", n_out=x.shape[1], out_dtypes=(F32,),
                tm_pref=512, tn_pref=512, x=x, gate=gate, seq_len=seq_len)
    return xn, (res[1] if emit_v else None)


def _conv_layer(x, h, gate, w, hist, *, seq_len, mod_seq):
    d = x.shape[1]
    glu = matmul(h, w["w_in"], w["b_in"], "glu", col_offsets=(0, d), n_out=d, out_dtypes=(F32,),
                 tm_pref=1024, tn_pref=512)
    act = conv_core(glu, hist, w["dw"], w["dw_b"], w["ln_g"], w["ln_b"], seq_len=seq_len)
    xn = matmul(act, w["w_out"], w["b_out"], "residual", n_out=d, out_dtypes=(F32,),
                tm_pref=1024, tn_pref=512, x=x, gate=gate, seq_len=mod_seq)
    return xn, glu


def _attn_layer(x, h, gate, w, kb_fn, *, seq_len, q_dim, head_dim, window, mask_front):
    d = x.shape[1]
    n_qkv = w["w_qkv"].shape[1]
    qkv = matmul(h, w["w_qkv"], w["b_qkv"], "plain", n_out=n_qkv, out_dtypes=(F32,),
                 tm_pref=1024, tn_pref=512)
    kv_dim = (n_qkv - q_dim) // 2
    k = qkv[:, q_dim:q_dim + kv_dim]
    v = qkv[:, q_dim + kv_dim:]
    kb, vb, k_win, v_win = kb_fn(k, v)
    o = attention(qkv, kb, vb, w["sinks"], q_dim=q_dim, head_dim=head_dim, window=window,
                  mask_front=mask_front)
    xn = matmul(o, w["w_o"], None, "residual", n_out=d, out_dtypes=(F32,),
                tm_pref=1024, tn_pref=512, x=x, gate=gate, seq_len=seq_len)
    return xn, k_win, v_win


def kernel(x_prompt, x_sample, cache_k_win, cache_v_win, state_conv, c_prompt, c_sample, norm_mix_g, norm_ch_g, norm_final_g, ada_w, ada_b, sgu_w_in, sgu_b_in, sgu_ln_g, sgu_ln_b, sgu_w_s, sgu_b_s, sgu_w_out, conv_w_in, conv_b_in, conv_dw, conv_dw_b, conv_ln_g, conv_ln_b, conv_w_out, conv_b_out, attn_w_qkv, attn_b_qkv, attn_sinks, attn_w_o, peer_w_q, peer_subkeys, peer_u, peer_v):
    bp, sp, d = x_prompt.shape
    bs, ss, _ = x_sample.shape
    depth = ada_w.shape[0]
    window, kv_heads, head_dim = cache_k_win.shape[2:]
    kv_dim = kv_heads * head_dim
    q_dim = attn_w_qkv.shape[2] - 2 * kv_dim
    conv_w = conv_dw.shape[1]
    peer_heads = peer_subkeys.shape[1]
    tp, ts = bp * sp, bs * ss
    assert ts == SGU_CHUNK and sp % SGU_CHUNK == 0 and sp % CHUNK == 0 and window % CHUNK == 0

    n_c = bp + bs
    rows = -(-n_c // 8) * 8
    c_all = jnp.concatenate([c_prompt, c_sample, jnp.zeros((rows - n_c, d), F32)], axis=0)
    mods = ada_all(c_all, ada_w, ada_b)

    def mod_vectors(layer):
        m = mods[layer].reshape(rows, 6, d)
        prompt = [m[:bp, k][:, None, :] for k in range(6)]
        sample = [jnp.repeat(m[bp:n_c, k], ss, axis=0)[None] for k in range(6)]
        return prompt, sample

    t_idx = jnp.arange(SGU_CHUNK)
    mask_p = ((t_idx[None, :] // CHUNK) <= (t_idx[:, None] // CHUNK)).astype(F32)
    blk = t_idx // ss
    mask_s = ((blk[None, :] == blk[:, None])
              & ((t_idx[None, :] % ss) // CHUNK <= (t_idx[:, None] % ss) // CHUNK)).astype(F32)

    xp = x_prompt.reshape(tp, d)
    xs = x_sample.reshape(ts, d)
    ia = ib = ic = 0
    conv_p, conv_s, kwin_p, vwin_p, kwin_s, vwin_s, sgu_s = [], [], [], [], [], [], []
    for layer in range(depth):
        mp, ms = mod_vectors(layer)
        hp = norm(xp, norm_mix_g[layer], mp[1], mp[0], seq_len=sp)
        hs = norm(xs, norm_mix_g[layer], ms[1], ms[0], seq_len=ts)
        kind = layer % N_MIXERS
        if kind == 0:
            w = dict(w_in=sgu_w_in[ia].astype(BF16), b_in=sgu_b_in[ia], ln_g=sgu_ln_g[ia],
                     ln_b=sgu_ln_b[ia], w_out=sgu_w_out[ia].astype(BF16))
            ws = sgu_w_s[ia]
            wp = dict(w, ws=ws, mask=mask_p, bst=sgu_b_s[ia].T)
            reps = SGU_CHUNK // ss
            wsm = dict(w, ws=jnp.tile(ws[:, :ss, :ss], (1, reps, reps)), mask=mask_s,
                       bst=jnp.tile(sgu_b_s[ia][:, :ss], (1, reps)).T)
            xp, _ = _sgu_layer(xp, hp, mp[2], wp, seq_len=sp, emit_v=False)
            xs, v_rows = _sgu_layer(xs, hs, ms[2], wsm, seq_len=ts, emit_v=True)
            sgu_s.append(v_rows.reshape(bs, ss, -1))
            ia += 1
        elif kind == 1:
            dw = conv_dw[ib]
            w = dict(w_in=conv_w_in[ib].astype(BF16), b_in=conv_b_in[ib], dw=dw, dw_b=conv_dw_b[ib],
                     ln_g=conv_ln_g[ib], ln_b=conv_ln_b[ib], w_out=conv_w_out[ib].astype(BF16),
                     b_out=conv_b_out[ib])
            pad = HALO - (conv_w - 1)
            hist_p = jnp.zeros((bp, HALO, d), F32)
            hist_s = jnp.pad(state_conv[ib], ((0, 0), (pad, 0), (0, 0)))
            xp, glu_p = _conv_layer(xp, hp, mp[2], w, hist_p, seq_len=sp, mod_seq=sp)
            xs, glu_s = _conv_layer(xs, hs, ms[2], w, hist_s, seq_len=ss, mod_seq=ts)
            conv_p.append(glu_p.reshape(bp, sp, d)[:, sp - (conv_w - 1):])
            conv_s.append(jnp.concatenate([state_conv[ib], glu_s.reshape(bs, ss, d)], axis=1)[:, -(conv_w - 1):])
            ib += 1
        else:
            w = dict(w_qkv=attn_w_qkv[ic].astype(BF16), b_qkv=attn_b_qkv[ic], sinks=attn_sinks[ic],
                     w_o=attn_w_o[ic].astype(BF16))
            n_chunks = sp // CHUNK
            band_prev = window // CHUNK

            def bands_prompt(k, v):
                def band(a):
                    a = a.reshape(bp, sp, kv_dim)
                    ap = jnp.pad(a, ((0, 0), (window, 0), (0, 0))).reshape(bp, n_chunks + band_prev, CHUNK, kv_dim)
                    ab = jnp.concatenate([ap[:, j:j + n_chunks] for j in range(band_prev + 1)], axis=2)
                    return ab, a[:, sp - window:].reshape(bp, window, kv_heads, head_dim)
                kb, k_win = band(k)
                vb, v_win = band(v)
                return kb, vb, k_win, v_win

            def bands_sample(k, v, ck=cache_k_win[ic], cv=cache_v_win[ic]):
                def band(a, cache):
                    aa = jnp.concatenate([cache.reshape(bs, window, kv_dim), a.reshape(bs, ss, kv_dim)], axis=1)
                    return aa[:, None], aa[:, -window:].reshape(bs, window, kv_heads, head_dim)
                kb, k_win = band(k, ck)
                vb, v_win = band(v, cv)
                return kb, vb, k_win, v_win

            common = dict(q_dim=q_dim, head_dim=head_dim, window=window)
            xp, kp, vp = _attn_layer(xp, hp, mp[2], w, bands_prompt, seq_len=sp, mask_front=True, **common)
            xs, ks, vs = _attn_layer(xs, hs, ms[2], w, bands_sample, seq_len=ts, mask_front=False, **common)
            kwin_p.append(kp)
            vwin_p.append(vp)
            kwin_s.append(ks)
            vwin_s.append(vs)
            ic += 1

        sk = peer_subkeys[layer]
        nk, half = sk.shape[2:]
        z = jnp.zeros((peer_heads, nk, half), F32)
        skbd = jnp.concatenate([jnp.concatenate([sk[:, 0], z], axis=2),
                                jnp.concatenate([z, sk[:, 1]], axis=2)], axis=1).astype(BF16)
        pw = dict(wqt=peer_w_q[layer].T.astype(BF16), skbd=skbd,
                  u=peer_u[layer].astype(BF16), vt=peer_v[layer].T.astype(BF16))
        xp = _peer_layer(xp, norm_ch_g[layer], mp[4], mp[3], mp[5], pw, seq_len=sp)
        xs = _peer_layer(xs, norm_ch_g[layer], ms[4], ms[3], ms[5], pw, seq_len=ts)

    y_prompt = norm(xp, norm_final_g, seq_len=sp, out_dtype=F32).reshape(bp, sp, d)
    y_sample = norm(xs, norm_final_g, seq_len=ts, out_dtype=F32).reshape(bs, ss, d)
    return (y_prompt, y_sample, jnp.stack(conv_p), jnp.stack(kwin_p), jnp.stack(vwin_p),
            jnp.stack(conv_s), jnp.stack(kwin_s), jnp.stack(vwin_s), jnp.stack(sgu_s))
```

```python
import functools

import jax
import jax.numpy as jnp
from jax import lax
from jax.experimental import pallas as pl
from jax.experimental.pallas import tpu as pltpu

F32 = jnp.float32
BF16 = jnp.bfloat16

EPS = 1e-6
CHUNK = 64
SGU_CHUNK = 128
SGU_GROUPS = 8
PEER_TOPK = 16
NEG_INF = -1e30
N_MIXERS = 3

V7X_VMEM_BYTES = 64 * 1024 * 1024
VMEM_LIMIT = V7X_VMEM_BYTES - 8 * 1024 * 1024
LANE = 128


def _cp(*sem):
    return pltpu.CompilerParams(dimension_semantics=sem, vmem_limit_bytes=VMEM_LIMIT)


def _tile(n, pref, mult=LANE):
    if n <= pref:
        return n
    t = (pref // mult) * mult
    while t >= mult:
        if n % t == 0:
            return t
        t -= mult
    return n


def _gelu(x):
    return jax.nn.gelu(x, approximate=True)


def _ada_kernel(c_ref, w_ref, b_ref, o_ref):
    c = c_ref[...]
    a = (c * jax.nn.sigmoid(c)).astype(BF16)
    o_ref[0] = jnp.dot(a, w_ref[0].astype(BF16), preferred_element_type=F32) + b_ref[0]


def ada_all(c_all, ada_w, ada_b):
    n_layers, d, n = ada_w.shape
    rows = c_all.shape[0]
    tn = _tile(n, 1024)
    return pl.pallas_call(
        _ada_kernel,
        grid=(n_layers, n // tn),
        in_specs=[pl.BlockSpec((rows, d), lambda l, j: (0, 0)),
                  pl.BlockSpec((1, d, tn), lambda l, j: (l, 0, j)),
                  pl.BlockSpec((1, 1, tn), lambda l, j: (l, 0, j))],
        out_specs=pl.BlockSpec((1, rows, tn), lambda l, j: (l, 0, j)),
        out_shape=jax.ShapeDtypeStruct((n_layers, rows, n), F32),
        compiler_params=_cp("arbitrary", "arbitrary"),
        name="ada",
    )(c_all, ada_w, ada_b.reshape(n_layers, 1, n))


def _mod_spec(arr, tm, seq_len, tn=None, col_of=None):
    _, r, d = arr.shape
    if tn is None:
        return pl.BlockSpec((1, r, d), lambda i, *_: ((i * tm) // seq_len, 0, 0))
    return pl.BlockSpec((1, r, tn), lambda i, j: ((i * tm) // seq_len, 0, col_of(j)))


def _norm_kernel(*refs, modulated, transposed):
    if modulated:
        x_ref, g_ref, sc_ref, sh_ref, o_ref = refs
    else:
        x_ref, g_ref, o_ref = refs
    x = x_ref[...]
    y = x * lax.rsqrt(jnp.mean(x * x, axis=-1, keepdims=True) + EPS) * g_ref[...]
    if modulated:
        y = y * (1.0 + sc_ref[0]) + sh_ref[0]
    if transposed:
        o_ref[...] = y.T.astype(o_ref.dtype)
    else:
        o_ref[...] = y.astype(o_ref.dtype)


def norm(x, g, scale=None, shift=None, *, seq_len, transposed=False, out_dtype=BF16):
    t, d = x.shape
    tm = _tile(seq_len, 512)
    modulated = scale is not None
    in_specs = [pl.BlockSpec((tm, d), lambda i: (i, 0)), pl.BlockSpec((1, d), lambda i: (0, 0))]
    args = [x, g.reshape(1, d)]
    if modulated:
        in_specs += [_mod_spec(scale, tm, seq_len), _mod_spec(shift, tm, seq_len)]
        args += [scale, shift]
    if transposed:
        out_spec = pl.BlockSpec((d, tm), lambda i: (0, i))
        out_shape = jax.ShapeDtypeStruct((d, t), out_dtype)
    else:
        out_spec = pl.BlockSpec((tm, d), lambda i: (i, 0))
        out_shape = jax.ShapeDtypeStruct((t, d), out_dtype)
    return pl.pallas_call(
        functools.partial(_norm_kernel, modulated=modulated, transposed=transposed),
        grid=(t // tm,), in_specs=in_specs, out_specs=out_spec, out_shape=out_shape,
        compiler_params=_cp("arbitrary"), name="norm",
    )(*args)


def _mm_kernel(*refs, n_w, has_bias, epilogue):
    h_ref = refs[0]
    w_refs = refs[1:1 + n_w]
    pos = 1 + n_w
    b_refs = refs[pos:pos + n_w] if has_bias else ()
    pos += n_w if has_bias else 0
    h = h_ref[...]
    accs = []
    for k in range(n_w):
        acc = jnp.dot(h, w_refs[k][...], preferred_element_type=F32)
        if has_bias:
            acc = acc + b_refs[k][...]
        accs.append(acc)
    rest = refs[pos:]
    if epilogue == "sgu_in":
        rest[0][...] = _gelu(accs[0]).astype(BF16)
        rest[1][...] = _gelu(accs[1])
    elif epilogue == "glu":
        rest[0][...] = accs[0] * jax.nn.sigmoid(accs[1])
    elif epilogue == "plain":
        rest[0][...] = accs[0]
    elif epilogue == "residual":
        x_ref, gate_ref, o_ref = rest
        o_ref[...] = x_ref[...] + gate_ref[0] * accs[0]
    else:
        raise ValueError(epilogue)


def matmul(h, w, bias, epilogue, *, col_offsets=(0,), n_out, out_dtypes, tm_pref, tn_pref,
           x=None, gate=None, seq_len=None):
    t, k_dim = h.shape
    tm = _tile(t if seq_len is None else seq_len, tm_pref, 8)
    tn = _tile(n_out, tn_pref)
    n_w = len(col_offsets)
    has_bias = bias is not None
    in_specs = [pl.BlockSpec((tm, k_dim), lambda i, j: (i, 0))]
    args = [h]
    for off in col_offsets:
        assert off % tn == 0
        in_specs.append(pl.BlockSpec((k_dim, tn), lambda i, j, o=off // tn: (0, j + o)))
        args.append(w)
    if has_bias:
        b2 = bias.reshape(1, -1)
        for off in col_offsets:
            in_specs.append(pl.BlockSpec((1, tn), lambda i, j, o=off // tn: (0, j + o)))
            args.append(b2)
    if epilogue == "residual":
        in_specs += [pl.BlockSpec((tm, tn), lambda i, j: (i, j)),
                     _mod_spec(gate, tm, seq_len, tn, lambda j: j)]
        args += [x, gate]
    out_specs = [pl.BlockSpec((tm, tn), lambda i, j: (i, j)) for _ in out_dtypes]
    out_shape = [jax.ShapeDtypeStruct((t, n_out), dt) for dt in out_dtypes]
    res = pl.pallas_call(
        functools.partial(_mm_kernel, n_w=n_w, has_bias=has_bias, epilogue=epilogue),
        grid=(t // tm, n_out // tn), in_specs=in_specs, out_specs=out_specs, out_shape=out_shape,
        compiler_params=_cp("arbitrary", "arbitrary"), name="mm_" + epilogue,
    )(*args)
    return res if len(res) > 1 else res[0]


def _sgu_mix_kernel(v_ref, u_ref, lg_ref, lb_ref, ws_ref, mask_ref, bst_ref, o_ref, *vout,
                    groups):
    v = v_ref[...]
    mu = jnp.mean(v, axis=-1, keepdims=True)
    vc = v - mu
    var = jnp.mean(vc * vc, axis=-1, keepdims=True)
    vn = vc * lax.rsqrt(var + EPS) * lg_ref[...] + lb_ref[...]
    if vout:
        vout[0][...] = vn
    eg = v.shape[1] // groups
    for g in range(groups):
        w = (ws_ref[g] * mask_ref[...]).astype(BF16)
        vg = vn[:, g * eg:(g + 1) * eg].astype(BF16)
        mixed = jnp.dot(w, vg, preferred_element_type=F32) + bst_ref[:, g:g + 1]
        u = u_ref[:, g * eg:(g + 1) * eg].astype(F32)
        o_ref[:, g * eg:(g + 1) * eg] = (u * mixed).astype(BF16)


def sgu_mix(v, u, ln_g, ln_b, ws, mask, bst, *, emit_v):
    t, e = v.shape
    n = ws.shape[1]
    groups = ws.shape[0]
    row = lambda i: (i, 0)
    full2 = lambda i: (0, 0)
    out_specs = [pl.BlockSpec((n, e), row)]
    out_shape = [jax.ShapeDtypeStruct((t, e), BF16)]
    if emit_v:
        out_specs.append(pl.BlockSpec((n, e), row))
        out_shape.append(jax.ShapeDtypeStruct((t, e), F32))
    return pl.pallas_call(
        functools.partial(_sgu_mix_kernel, groups=groups),
        grid=(t // n,),
        in_specs=[pl.BlockSpec((n, e), row), pl.BlockSpec((n, e), row),
                  pl.BlockSpec((1, e), full2), pl.BlockSpec((1, e), full2),
                  pl.BlockSpec((groups, n, n), lambda i: (0, 0, 0)),
                  pl.BlockSpec((n, n), full2), pl.BlockSpec((n, groups), full2)],
        out_specs=out_specs, out_shape=out_shape,
        compiler_params=_cp("arbitrary"), name="sgu_mix",
    )(v, u, ln_g.reshape(1, e), ln_b.reshape(1, e), ws, mask, bst)


HALO = 32
CONV_COLS = 256


def _conv_kernel(*refs, tt, width, single):
    if single:
        cur_ref, hist_ref, dw_ref, dwb_ref, lg_ref, lb_ref, o_ref, xp_ref, y_ref = refs
    else:
        cur_ref, prev_ref, hist_ref, dw_ref, dwb_ref, lg_ref, lb_ref, o_ref, xp_ref, y_ref = refs
    d = cur_ref.shape[1]
    xp_ref[HALO:HALO + tt, :] = cur_ref[...]
    if single:
        xp_ref[0:HALO, :] = hist_ref[0]
    else:
        first = pl.program_id(1) == 0

        @pl.when(first)
        def _():
            xp_ref[0:HALO, :] = hist_ref[0]

        @pl.when(jnp.logical_not(first))
        def _():
            xp_ref[0:HALO, :] = prev_ref[...]

    base = HALO - (width - 1)
    cw = min(CONV_COLS, d)
    for c in range(d // cw):
        cols = slice(c * cw, (c + 1) * cw)
        acc = jnp.zeros((tt, cw), F32)
        for j in range(width):
            acc = acc + dw_ref[j:j + 1, cols] * xp_ref[base + j:base + j + tt, cols]
        y_ref[:, cols] = acc + dwb_ref[:, cols]
    y = y_ref[...]
    mu = jnp.mean(y, axis=-1, keepdims=True)
    yc = y - mu
    var = jnp.mean(yc * yc, axis=-1, keepdims=True)
    yn = yc * lax.rsqrt(var + EPS) * lg_ref[...] + lb_ref[...]
    o_ref[...] = (yn * jax.nn.sigmoid(yn)).astype(BF16)


def conv_core(glu, hist, dw, dw_b, ln_g, ln_b, *, seq_len):
    t, d = glu.shape
    b = t // seq_len
    width = dw.shape[0]
    tt = _tile(seq_len, 128, HALO)
    single = tt == seq_len
    nb = seq_len // tt
    vec = lambda a: a.reshape(1, d)
    full2 = lambda bi, i: (0, 0)
    in_specs = [pl.BlockSpec((tt, d), lambda bi, i: (bi * nb + i, 0))]
    args = [glu]
    if not single:
        per = tt // HALO
        in_specs.append(pl.BlockSpec((HALO, d), lambda bi, i: (jnp.maximum((bi * nb + i) * per - 1, 0), 0)))
        args.append(glu)
    in_specs += [pl.BlockSpec((1, HALO, d), lambda bi, i: (bi, 0, 0)),
                 pl.BlockSpec((width, d), full2),
                 pl.BlockSpec((1, d), full2), pl.BlockSpec((1, d), full2), pl.BlockSpec((1, d), full2)]
    args += [hist, dw, vec(dw_b), vec(ln_g), vec(ln_b)]
    return pl.pallas_call(
        functools.partial(_conv_kernel, tt=tt, width=width, single=single),
        grid=(b, nb), in_specs=in_specs,
        out_specs=pl.BlockSpec((tt, d), lambda bi, i: (bi * nb + i, 0)),
        out_shape=jax.ShapeDtypeStruct((t, d), BF16),
        scratch_shapes=[pltpu.VMEM((HALO + tt, d), F32), pltpu.VMEM((tt, d), F32)],
        compiler_params=_cp("arbitrary", "arbitrary"), name="conv_core",
    )(*args)


def _attn_kernel(sink_ref, q_ref, *rest, n_pieces, kv_heads, q_per_kv, head_dim, window, n_heads,
                 mask_front):
    k_refs = rest[:n_pieces]
    v_refs = rest[n_pieces:2 * n_pieces]
    o_ref, k_scr, v_scr = rest[2 * n_pieces:]
    off = 0
    for kr, vr in zip(k_refs, v_refs):
        rows = kr.shape[-2]
        k_scr[off:off + rows, :] = kr[...].reshape(kr.shape[-2:]).astype(BF16)
        v_scr[off:off + rows, :] = vr[...].reshape(vr.shape[-2:]).astype(BF16)
        off += rows
    nq = q_ref.shape[0]
    nk = k_scr.shape[0]
    qi = lax.broadcasted_iota(jnp.int32, (nq, nk), 0)
    kj = lax.broadcasted_iota(jnp.int32, (nq, nk), 1)
    dist = jnp.abs(window + qi - kj).astype(F32)
    if mask_front:
        valid = kj >= window - pl.program_id(1) * nq
    scale = head_dim ** -0.5
    groups = [range(g * q_per_kv, (g + 1) * q_per_kv) for g in range(kv_heads)]
    scores, sinks = [], []
    for g, heads in enumerate(groups):
        q = jnp.concatenate([q_ref[:, hd * head_dim:(hd + 1) * head_dim] for hd in heads], axis=0)
        k = k_scr[:, g * head_dim:(g + 1) * head_dim]
        s = lax.dot_general(q.astype(BF16), k, (((1,), (1,)), ((), ())), preferred_element_type=F32)
        bias = []
        for hd in heads:
            b = (2.0 ** (-8.0 * (hd + 1) / n_heads)) * dist
            bias.append(jnp.where(valid, b, -NEG_INF) if mask_front else b)
        scores.append(s * scale - jnp.concatenate(bias, axis=0))
        sinks.append(jnp.concatenate([jnp.full((nq, 1), sink_ref[hd], F32) for hd in heads], axis=0))
    maxes = [jnp.maximum(jnp.max(s, axis=-1, keepdims=True), sk) for s, sk in zip(scores, sinks)]
    probs = [jnp.exp(s - m) for s, m in zip(scores, maxes)]
    dens = [jnp.sum(p, axis=-1, keepdims=True) + jnp.exp(sk - m) for p, sk, m in zip(probs, sinks, maxes)]
    for g, heads in enumerate(groups):
        v = v_scr[:, g * head_dim:(g + 1) * head_dim]
        o = jnp.dot((probs[g] / dens[g]).astype(BF16), v, preferred_element_type=F32).astype(BF16)
        for r, hd in enumerate(heads):
            o_ref[:, hd * head_dim:(hd + 1) * head_dim] = o[r * nq:(r + 1) * nq]


def attention(qkv, k_pieces, v_pieces, sinks, *, grid, nq, q_dim, head_dim, window, mask_front):
    t = qkv.shape[0]
    nc = grid[1]
    kv_dim = k_pieces[0][1].block_shape[-1]
    band = sum(spec.block_shape[-2] for _, spec in k_pieces)
    n_heads = q_dim // head_dim
    kv_heads = kv_dim // head_dim
    kern = functools.partial(_attn_kernel, n_pieces=len(k_pieces), kv_heads=kv_heads,
                             q_per_kv=n_heads // kv_heads, head_dim=head_dim, window=window,
                             n_heads=n_heads, mask_front=mask_front)
    pieces = k_pieces + v_pieces
    return pl.pallas_call(
        kern, grid=grid,
        in_specs=[pl.BlockSpec(memory_space=pltpu.SMEM),
                  pl.BlockSpec((nq, q_dim), lambda bi, c: (bi * nc + c, 0))] + [sp for _, sp in pieces],
        out_specs=pl.BlockSpec((nq, q_dim), lambda bi, c: (bi * nc + c, 0)),
        out_shape=jax.ShapeDtypeStruct((t, q_dim), BF16),
        scratch_shapes=[pltpu.VMEM((band, kv_dim), BF16), pltpu.VMEM((band, kv_dim), BF16)],
        compiler_params=_cp("arbitrary", "arbitrary"), name="attention",
    )(sinks, qkv, *[a for a, _ in pieces])


def _topk_tables(s1, s2, a_scr, b_scr, c_scr, u, *, topk, n_cand):
    w = s1
    for r in range(topk):
        m = jnp.max(w, axis=0, keepdims=True)
        a_scr[u, r:r + 1, :] = m
        w = jnp.where(w == m, NEG_INF, w)
    w = s2
    rank2 = jnp.full(s2.shape, topk + 1.0, F32)
    for r in range(topk):
        m = jnp.max(w, axis=0, keepdims=True)
        b_scr[u, r:r + 1, :] = m
        hit = w == m
        w = jnp.where(hit, NEG_INF, w)
        rank2 = jnp.where(hit, r + 1.0, rank2)
    k = 0
    for r in range(topk):
        for j in range(topk // (r + 1)):
            c_scr[u, k:k + 1, :] = a_scr[u, r:r + 1, :] + b_scr[u, j:j + 1, :]
            k += 1
    rows = c_scr.shape[1]
    c_scr[u, n_cand:rows, :] = jnp.full((rows - n_cand, c_scr.shape[2]), NEG_INF, F32)
    cand = c_scr[u]
    ridx = lax.broadcasted_iota(jnp.int32, cand.shape, 0)
    m0 = jnp.max(cand, axis=0, keepdims=True)
    z = jnp.zeros_like(m0)
    m = m0
    for r in range(topk):
        m = jnp.max(cand, axis=0, keepdims=True)
        z = z + jnp.exp(m - m0)
        first = jnp.min(jnp.where(cand == m, ridx, rows), axis=0, keepdims=True)
        cand = jnp.where(ridx == first, NEG_INF, cand)
    tau = m
    a = a_scr[u]
    cnt = jnp.zeros_like(s1)
    for j in range(topk):
        ok = a + b_scr[u, j:j + 1, :] >= tau
        theta = jnp.min(jnp.where(ok, a, -NEG_INF), axis=0, keepdims=True)
        cnt = jnp.where(s1 >= theta, j + 1.0, cnt)
    e1 = jnp.exp(s1 - a_scr[u, 0:1, :])
    e2 = jnp.exp(s2 - b_scr[u, 0:1, :]) / z
    return cnt, e1, rank2, e2


def _peer_route_kernel(ht_ref, wqt_ref, sk_ref, cnt_ref, e1_ref, r2_ref, e2_ref, a_scr, b_scr, c_scr,
                       *, topk, n_cand, hps):
    hk = sk_ref.shape[1]
    nk = hk // 2
    nl = ht_ref.shape[1] // LANE
    scores = []
    for hh in range(hps):
        qt = jnp.dot(wqt_ref[hh * hk:(hh + 1) * hk, :], ht_ref[...], preferred_element_type=F32)
        scores.append(jnp.dot(sk_ref[hh], qt.astype(BF16), preferred_element_type=F32))
    for hh in range(hps):
        for l in range(nl):
            lanes = slice(l * LANE, (l + 1) * LANE)
            cnt, e1, rank2, e2 = _topk_tables(scores[hh][0:nk, lanes], scores[hh][nk:hk, lanes],
                                              a_scr, b_scr, c_scr, hh * nl + l, topk=topk, n_cand=n_cand)
            cnt_ref[hh, l] = cnt
            e1_ref[hh, l] = e1
            r2_ref[hh, l] = rank2.astype(r2_ref.dtype)
            e2_ref[hh, l] = e2.astype(e2_ref.dtype)


def peer_route(ht, wqt, skbd):
    d, t = ht.shape
    heads, hk, _ = skbd.shape
    nk = hk // 2
    tt = _tile(t, 256)
    nl = tt // LANE
    hps = 4
    n_cand = sum(PEER_TOPK // (r + 1) for r in range(PEER_TOPK))
    cand_rows = -(-n_cand // 8) * 8
    units = hps * nl
    out32 = jax.ShapeDtypeStruct((heads, t // LANE, nk, LANE), F32)
    out16 = jax.ShapeDtypeStruct((heads, t // LANE, nk, LANE), BF16)
    spec = pl.BlockSpec((hps, nl, nk, LANE), lambda i, h: (h, i, 0, 0))
    return pl.pallas_call(
        functools.partial(_peer_route_kernel, topk=PEER_TOPK, n_cand=n_cand, hps=hps),
        grid=(t // tt, heads // hps),
        in_specs=[pl.BlockSpec((d, tt), lambda i, h: (0, i)),
                  pl.BlockSpec((hps * hk, d), lambda i, h: (h, 0)),
                  pl.BlockSpec((hps, hk, hk), lambda i, h: (h, 0, 0))],
        out_specs=[spec, spec, spec, spec], out_shape=[out32, out32, out16, out16],
        scratch_shapes=[pltpu.VMEM((units, PEER_TOPK, LANE), F32), pltpu.VMEM((units, PEER_TOPK, LANE), F32),
                        pltpu.VMEM((units, cand_rows, LANE), F32)],
        compiler_params=_cp("arbitrary", "arbitrary"), name="peer_route",
    )(ht, wqt, skbd)


def _peer_dense_kernel(ht_ref, u_ref, vt_ref, cnt_ref, e1_ref, r2_ref, e2_ref, o_ref, a_scr, w_scr,
                       *, heads, n_i1, nk, tb, tc):
    @pl.when(pl.program_id(1) == 0)
    def _():
        o_ref[...] = jnp.zeros_like(o_ref)

    def row_bf16(ref, h, l, il):
        return jnp.broadcast_to(ref[h, l, il:il + 1, :], (nk, LANE)).astype(BF16)

    lpc = tc // LANE
    ec, d = u_ref.shape
    for c in range(tb // tc):
        cols = slice(c * tc, (c + 1) * tc)
        for half in (slice(0, ec // 2), slice(ec // 2, ec)):
            a_scr[c, half, :] = jnp.dot(u_ref[half, :], ht_ref[:, cols], preferred_element_type=F32)
    for c in range(tb // tc):
        cols = slice(c * tc, (c + 1) * tc)
        for il in range(n_i1):
            rows = slice(il * nk, (il + 1) * nk)
            parts = []
            for l in range(c * lpc, (c + 1) * lpc):
                gate = None
                for h in range(heads):
                    term = jnp.where(row_bf16(cnt_ref, h, l, il) >= r2_ref[h, l],
                                     row_bf16(e1_ref, h, l, il), 0.0) * e2_ref[h, l]
                    gate = term if gate is None else gate + term
                sub = slice((l - c * lpc) * LANE, (l - c * lpc + 1) * LANE)
                parts.append(_gelu(a_scr[c, rows, sub]).astype(BF16) * gate)
            w_scr[c, rows, :] = jnp.concatenate(parts, axis=1)
        for half in (slice(0, d // 2), slice(d // 2, d)):
            o_ref[half, cols] += jnp.dot(vt_ref[half, :], w_scr[c], preferred_element_type=F32)


def peer_dense(ht, u_bf, vt_bf, cnt, e1, r2, e2):
    d, t = ht.shape
    n_exp = u_bf.shape[0]
    heads, _, nk, _ = cnt.shape
    tb = _tile(t, 1024)
    tc = _tile(tb, 256)
    nl = tb // LANE
    n_i1 = 8
    ec = n_i1 * nk
    once = pl.Buffered(1)
    return pl.pallas_call(
        functools.partial(_peer_dense_kernel, heads=heads, n_i1=n_i1, nk=nk, tb=tb, tc=tc),
        grid=(t // tb, n_exp // ec),
        in_specs=[pl.BlockSpec((d, tb), lambda i, j: (0, i), pipeline_mode=once),
                  pl.BlockSpec((ec, d), lambda i, j: (j, 0)),
                  pl.BlockSpec((d, ec), lambda i, j: (0, j)),
                  pl.BlockSpec((heads, nl, n_i1, LANE), lambda i, j: (0, i, j, 0)),
                  pl.BlockSpec((heads, nl, n_i1, LANE), lambda i, j: (0, i, j, 0)),
                  pl.BlockSpec((heads, nl, nk, LANE), lambda i, j: (0, i, 0, 0), pipeline_mode=once),
                  pl.BlockSpec((heads, nl, nk, LANE), lambda i, j: (0, i, 0, 0), pipeline_mode=once)],
        out_specs=pl.BlockSpec((d, tb), lambda i, j: (0, i)),
        out_shape=jax.ShapeDtypeStruct((d, t), F32),
        scratch_shapes=[pltpu.VMEM((tb // tc, ec, tc), F32), pltpu.VMEM((tb // tc, ec, tc), BF16)],
        compiler_params=_cp("arbitrary", "arbitrary"), name="peer_dense",
    )(ht, u_bf, vt_bf, cnt, e1, r2, e2)


def _resid_t_kernel(x_ref, yt_ref, gate_ref, o_ref):
    o_ref[...] = x_ref[...] + gate_ref[0] * yt_ref[...].T


def residual_t(x, yt, gate, *, seq_len):
    t, d = x.shape
    tm = _tile(seq_len, 512)
    return pl.pallas_call(
        _resid_t_kernel, grid=(t // tm,),
        in_specs=[pl.BlockSpec((tm, d), lambda i: (i, 0)), pl.BlockSpec((d, tm), lambda i: (0, i)),
                  _mod_spec(gate, tm, seq_len)],
        out_specs=pl.BlockSpec((tm, d), lambda i: (i, 0)),
        out_shape=jax.ShapeDtypeStruct((t, d), F32),
        compiler_params=_cp("arbitrary"), name="residual_t",
    )(x, yt, gate)


def _peer_layer(x, g, sc, sh, gate, pw, *, seq_len):
    ht = norm(x, g, sc, sh, seq_len=seq_len, transposed=True)
    cnt, e1, r2, e2 = peer_route(ht, pw["wqt"], pw["skbd"])
    yt = peer_dense(ht, pw["u"], pw["vt"], cnt, e1, r2, e2)
    return residual_t(x, yt, gate, seq_len=seq_len)


def _sgu_layer(x, h, gate, w, *, seq_len, emit_v):
    e = w["w_out"].shape[0]
    u, v = matmul(h, w["w_in"], w["b_in"], "sgu_in", col_offsets=(0, e), n_out=e,
                  out_dtypes=(BF16, F32), tm_pref=1024, tn_pref=512)
    res = sgu_mix(v, u, w["ln_g"], w["ln_b"], w["ws"], w["mask"], w["bst"], emit_v=emit_v)
    gated = res[0]
    xn = matmul(gated, w["w_out"], None, "residual", n_out=x.shape[1], out_dtypes=(F32,),
                tm_pref=512, tn_pref=512, x=x, gate=gate, seq_len=seq_len)
    return xn, (res[1] if emit_v else None)


def _conv_layer(x, h, gate, w, hist, *, seq_len, mod_seq):
    d = x.shape[1]
    glu = matmul(h, w["w_in"], w["b_in"], "glu", col_offsets=(0, d), n_out=d, out_dtypes=(F32,),
                 tm_pref=1024, tn_pref=512)
    act = conv_core(glu, hist, w["dw"], w["dw_b"], w["ln_g"], w["ln_b"], seq_len=seq_len)
    xn = matmul(act, w["w_out"], w["b_out"], "residual", n_out=d, out_dtypes=(F32,),
                tm_pref=1024, tn_pref=512, x=x, gate=gate, seq_len=mod_seq)
    return xn, glu


def _attn_layer(x, h, gate, w, caches, *, batch, seq_len, mod_seq, q_dim, head_dim, window):
    d = x.shape[1]
    n_qkv = w["w_qkv"].shape[1]
    qkv = matmul(h, w["w_qkv"], w["b_qkv"], "plain", n_out=n_qkv, out_dtypes=(F32,),
                 tm_pref=1024, tn_pref=512)
    kv_dim = (n_qkv - q_dim) // 2
    assert q_dim % kv_dim == 0
    col = {"k": q_dim // kv_dim, "v": q_dim // kv_dim + 1}
    if caches is None:
        nq = CHUNK
        nc = seq_len // nq
        back = window // nq

        def pieces(which):
            return [(qkv, pl.BlockSpec((nq, kv_dim), lambda bi, c, j=j, cb=col[which]:
                                       (bi * nc + jnp.maximum(c - back + j, 0), cb)))
                    for j in range(back + 1)]
        k_pieces, v_pieces = pieces("k"), pieces("v")
    else:
        nq, nc = seq_len, 1

        def pieces(which, cache):
            return [(cache, pl.BlockSpec((1, window, kv_dim), lambda bi, c: (bi, 0, 0))),
                    (qkv, pl.BlockSpec((nq, kv_dim), lambda bi, c, cb=col[which]: (bi, cb)))]
        k_pieces, v_pieces = pieces("k", caches[0]), pieces("v", caches[1])
    o = attention(qkv, k_pieces, v_pieces, w["sinks"], grid=(batch, nc), nq=nq, q_dim=q_dim,
                  head_dim=head_dim, window=window, mask_front=caches is None)
    xn = matmul(o, w["w_o"], None, "residual", n_out=d, out_dtypes=(F32,),
                tm_pref=1024, tn_pref=512, x=x, gate=gate, seq_len=mod_seq)
    k_new = qkv[:, q_dim:q_dim + kv_dim].reshape(batch, seq_len, kv_dim)
    v_new = qkv[:, q_dim + kv_dim:].reshape(batch, seq_len, kv_dim)
    if caches is not None:
        k_new = jnp.concatenate([caches[0], k_new], axis=1)
        v_new = jnp.concatenate([caches[1], v_new], axis=1)
    return xn, k_new[:, -window:], v_new[:, -window:]


def kernel(x_prompt, x_sample, cache_k_win, cache_v_win, state_conv, c_prompt, c_sample, norm_mix_g, norm_ch_g, norm_final_g, ada_w, ada_b, sgu_w_in, sgu_b_in, sgu_ln_g, sgu_ln_b, sgu_w_s, sgu_b_s, sgu_w_out, conv_w_in, conv_b_in, conv_dw, conv_dw_b, conv_ln_g, conv_ln_b, conv_w_out, conv_b_out, attn_w_qkv, attn_b_qkv, attn_sinks, attn_w_o, peer_w_q, peer_subkeys, peer_u, peer_v):
    bp, sp, d = x_prompt.shape
    bs, ss, _ = x_sample.shape
    depth = ada_w.shape[0]
    window, kv_heads, head_dim = cache_k_win.shape[2:]
    kv_dim = kv_heads * head_dim
    q_dim = attn_w_qkv.shape[2] - 2 * kv_dim
    conv_w = conv_dw.shape[1]
    peer_heads = peer_subkeys.shape[1]
    tp, ts = bp * sp, bs * ss
    assert ts == SGU_CHUNK and sp % SGU_CHUNK == 0 and sp % CHUNK == 0 and window % CHUNK == 0

    n_c = bp + bs
    rows = -(-n_c // 8) * 8
    c_all = jnp.concatenate([c_prompt, c_sample, jnp.zeros((rows - n_c, d), F32)], axis=0)
    mods = ada_all(c_all, ada_w, ada_b)

    def mod_vectors(layer):
        m = mods[layer].reshape(rows, 6, d)
        prompt = [m[:bp, k][:, None, :] for k in range(6)]
        sample = [jnp.repeat(m[bp:n_c, k], ss, axis=0)[None] for k in range(6)]
        return prompt, sample

    t_idx = jnp.arange(SGU_CHUNK)
    mask_p = ((t_idx[None, :] // CHUNK) <= (t_idx[:, None] // CHUNK)).astype(F32)
    blk = t_idx // ss
    mask_s = ((blk[None, :] == blk[:, None])
              & ((t_idx[None, :] % ss) // CHUNK <= (t_idx[:, None] % ss) // CHUNK)).astype(F32)

    xp = x_prompt.reshape(tp, d)
    xs = x_sample.reshape(ts, d)
    ia = ib = ic = 0
    conv_p, conv_s, kwin_p, vwin_p, kwin_s, vwin_s, sgu_s = [], [], [], [], [], [], []
    for layer in range(depth):
        mp, ms = mod_vectors(layer)
        hp = norm(xp, norm_mix_g[layer], mp[1], mp[0], seq_len=sp)
        hs = norm(xs, norm_mix_g[layer], ms[1], ms[0], seq_len=ts)
        kind = layer % N_MIXERS
        if kind == 0:
            w = dict(w_in=sgu_w_in[ia].astype(BF16), b_in=sgu_b_in[ia], ln_g=sgu_ln_g[ia],
                     ln_b=sgu_ln_b[ia], w_out=sgu_w_out[ia].astype(BF16))
            ws = sgu_w_s[ia]
            wp = dict(w, ws=ws, mask=mask_p, bst=sgu_b_s[ia].T)
            reps = SGU_CHUNK // ss
            wsm = dict(w, ws=jnp.tile(ws[:, :ss, :ss], (1, reps, reps)), mask=mask_s,
                       bst=jnp.tile(sgu_b_s[ia][:, :ss], (1, reps)).T)
            xp, _ = _sgu_layer(xp, hp, mp[2], wp, seq_len=sp, emit_v=False)
            xs, v_rows = _sgu_layer(xs, hs, ms[2], wsm, seq_len=ts, emit_v=True)
            sgu_s.append(v_rows.reshape(bs, ss, -1))
            ia += 1
        elif kind == 1:
            dw = conv_dw[ib]
            w = dict(w_in=conv_w_in[ib].astype(BF16), b_in=conv_b_in[ib], dw=dw, dw_b=conv_dw_b[ib],
                     ln_g=conv_ln_g[ib], ln_b=conv_ln_b[ib], w_out=conv_w_out[ib].astype(BF16),
                     b_out=conv_b_out[ib])
            pad = HALO - (conv_w - 1)
            hist_p = jnp.zeros((bp, HALO, d), F32)
            hist_s = jnp.pad(state_conv[ib], ((0, 0), (pad, 0), (0, 0)))
            xp, glu_p = _conv_layer(xp, hp, mp[2], w, hist_p, seq_len=sp, mod_seq=sp)
            xs, glu_s = _conv_layer(xs, hs, ms[2], w, hist_s, seq_len=ss, mod_seq=ts)
            conv_p.append(glu_p.reshape(bp, sp, d)[:, sp - (conv_w - 1):])
            conv_s.append(jnp.concatenate([state_conv[ib], glu_s.reshape(bs, ss, d)], axis=1)[:, -(conv_w - 1):])
            ib += 1
        else:
            w = dict(w_qkv=attn_w_qkv[ic].astype(BF16), b_qkv=attn_b_qkv[ic], sinks=attn_sinks[ic],
                     w_o=attn_w_o[ic].astype(BF16))
            caches = (cache_k_win[ic].reshape(bs, window, kv_dim), cache_v_win[ic].reshape(bs, window, kv_dim))
            common = dict(q_dim=q_dim, head_dim=head_dim, window=window)
            xp, kp, vp = _attn_layer(xp, hp, mp[2], w, None, batch=bp, seq_len=sp, mod_seq=sp, **common)
            xs, ks, vs = _attn_layer(xs, hs, ms[2], w, caches, batch=bs, seq_len=ss, mod_seq=ts, **common)
            kwin_p.append(kp.reshape(bp, window, kv_heads, head_dim))
            vwin_p.append(vp.reshape(bp, window, kv_heads, head_dim))
            kwin_s.append(ks.reshape(bs, window, kv_heads, head_dim))
            vwin_s.append(vs.reshape(bs, window, kv_heads, head_dim))
            ic += 1

        sk = peer_subkeys[layer]
        nk, half = sk.shape[2:]
        z = jnp.zeros((peer_heads, nk, half), F32)
        skbd = jnp.concatenate([jnp.concatenate([sk[:, 0], z], axis=2),
                                jnp.concatenate([z, sk[:, 1]], axis=2)], axis=1).astype(BF16)
        pw = dict(wqt=peer_w_q[layer].T.astype(BF16), skbd=skbd,
                  u=peer_u[layer].astype(BF16), vt=peer_v[layer].T.astype(BF16))
        xp = _peer_layer(xp, norm_ch_g[layer], mp[4], mp[3], mp[5], pw, seq_len=sp)
        xs = _peer_layer(xs, norm_ch_g[layer], ms[4], ms[3], ms[5], pw, seq_len=ts)

    y_prompt = norm(xp, norm_final_g, seq_len=sp, out_dtype=F32).reshape(bp, sp, d)
    y_sample = norm(xs, norm_final_g, seq_len=ts, out_dtype=F32).reshape(bs, ss, d)
    return (y_prompt, y_sample, jnp.stack(conv_p), jnp.stack(kwin_p), jnp.stack(vwin_p),
            jnp.stack(conv_s), jnp.stack(kwin_s), jnp.stack(vwin_s), jnp.stack(sgu_s))
```

```python
import functools

import jax
import jax.numpy as jnp
from jax import lax
from jax.experimental import pallas as pl
from jax.experimental.pallas import tpu as pltpu

F32 = jnp.float32
BF16 = jnp.bfloat16

EPS = 1e-6
CHUNK = 64
SGU_CHUNK = 128
SGU_GROUPS = 8
PEER_TOPK = 16
NEG_INF = -1e30
N_MIXERS = 3

V7X_VMEM_BYTES = 64 * 1024 * 1024
VMEM_LIMIT = V7X_VMEM_BYTES - 8 * 1024 * 1024
LANE = 128
SUBLANES = 8


def _cp(*sem):
    return pltpu.CompilerParams(dimension_semantics=sem, vmem_limit_bytes=VMEM_LIMIT)


def _tile(n, pref, mult=LANE):
    if n <= pref:
        return n
    t = (pref // mult) * mult
    while t >= mult:
        if n % t == 0:
            return t
        t -= mult
    return n


def _gelu(x):
    return jax.nn.gelu(x, approximate=True)


def _ada_kernel(c_ref, w_ref, b_ref, o_ref):
    c = c_ref[...]
    a = (c * jax.nn.sigmoid(c)).astype(BF16)
    o_ref[0] = jnp.dot(a, w_ref[0].astype(BF16), preferred_element_type=F32) + b_ref[0]


def ada_all(c_all, ada_w, ada_b):
    n_layers, d, n = ada_w.shape
    rows = c_all.shape[0]
    tn = _tile(n, 1024)
    return pl.pallas_call(
        _ada_kernel,
        grid=(n_layers, n // tn),
        in_specs=[pl.BlockSpec((rows, d), lambda l, j: (0, 0)),
                  pl.BlockSpec((1, d, tn), lambda l, j: (l, 0, j)),
                  pl.BlockSpec((1, 1, tn), lambda l, j: (l, 0, j))],
        out_specs=pl.BlockSpec((1, rows, tn), lambda l, j: (l, 0, j)),
        out_shape=jax.ShapeDtypeStruct((n_layers, rows, n), F32),
        compiler_params=_cp("arbitrary", "arbitrary"),
        name="ada",
    )(c_all, ada_w, ada_b.reshape(n_layers, 1, n))


def _mod_spec(arr, tm, seq_len, tn=None, col_of=None):
    _, r, d = arr.shape
    if tn is None:
        return pl.BlockSpec((1, r, d), lambda i, *_: ((i * tm) // seq_len, 0, 0))
    return pl.BlockSpec((1, r, tn), lambda i, j: ((i * tm) // seq_len, 0, col_of(j)))


def _norm_kernel(*refs, modulated, transposed, pending, emit_x):
    refs = list(refs)
    x = refs.pop(0)[...]
    if pending:
        yt_ref, pg_ref = refs.pop(0), refs.pop(0)
        x = x + pg_ref[0] * yt_ref[...].T
    g_ref = refs.pop(0)
    y = x * lax.rsqrt(jnp.mean(x * x, axis=-1, keepdims=True) + EPS) * g_ref[...]
    if modulated:
        sc_ref, sh_ref = refs.pop(0), refs.pop(0)
        y = y * (1.0 + sc_ref[0]) + sh_ref[0]
    o_ref = refs.pop(0)
    if transposed:
        o_ref[...] = y.T.astype(o_ref.dtype)
    else:
        o_ref[...] = y.astype(o_ref.dtype)
    if emit_x:
        refs.pop(0)[...] = x


def norm(x, g, scale=None, shift=None, *, seq_len, transposed=False, out_dtype=BF16, pending=None,
         emit_x=False):
    t, d = x.shape
    tm = _tile(seq_len, 512)
    modulated = scale is not None
    row = pl.BlockSpec((tm, d), lambda i: (i, 0))
    in_specs, args = [row], [x]
    if pending is not None:
        in_specs += [pl.BlockSpec((d, tm), lambda i: (0, i)), _mod_spec(pending[1], tm, seq_len)]
        args += list(pending)
    in_specs.append(pl.BlockSpec((1, d), lambda i: (0, 0)))
    args.append(g.reshape(1, d))
    if modulated:
        in_specs += [_mod_spec(scale, tm, seq_len), _mod_spec(shift, tm, seq_len)]
        args += [scale, shift]
    if transposed:
        out_specs = [pl.BlockSpec((d, tm), lambda i: (0, i))]
        out_shape = [jax.ShapeDtypeStruct((d, t), out_dtype)]
    else:
        out_specs = [row]
        out_shape = [jax.ShapeDtypeStruct((t, d), out_dtype)]
    if emit_x:
        out_specs.append(row)
        out_shape.append(jax.ShapeDtypeStruct((t, d), F32))
    res = pl.pallas_call(
        functools.partial(_norm_kernel, modulated=modulated, transposed=transposed,
                          pending=pending is not None, emit_x=emit_x),
        grid=(t // tm,), in_specs=in_specs, out_specs=out_specs, out_shape=out_shape,
        compiler_params=_cp("arbitrary"), name="norm",
    )(*args)
    return res if emit_x else res[0]


def _mm_kernel(*refs, n_w, has_bias, epilogue):
    h_ref = refs[0]
    w_refs = refs[1:1 + n_w]
    pos = 1 + n_w
    b_refs = refs[pos:pos + n_w] if has_bias else ()
    pos += n_w if has_bias else 0
    h = h_ref[...]
    accs = []
    for k in range(n_w):
        acc = jnp.dot(h, w_refs[k][...], preferred_element_type=F32)
        if has_bias:
            acc = acc + b_refs[k][...]
        accs.append(acc)
    rest = refs[pos:]
    if epilogue == "sgu_in":
        rest[0][...] = _gelu(accs[0]).astype(BF16)
        rest[1][...] = _gelu(accs[1])
    elif epilogue == "glu":
        rest[0][...] = accs[0] * jax.nn.sigmoid(accs[1])
    elif epilogue == "plain":
        rest[0][...] = accs[0]
    elif epilogue == "residual":
        x_ref, gate_ref, o_ref = rest
        o_ref[...] = x_ref[...] + gate_ref[0] * accs[0]
    else:
        raise ValueError(epilogue)


def matmul(h, w, bias, epilogue, *, col_offsets=(0,), n_out, out_dtypes, tm_pref, tn_pref,
           x=None, gate=None, seq_len=None):
    w, layer = w
    t, k_dim = h.shape
    tm = _tile(t if seq_len is None else seq_len, tm_pref, 8)
    tn = _tile(n_out, tn_pref)
    n_w = len(col_offsets)
    has_bias = bias is not None
    in_specs = [pl.BlockSpec((tm, k_dim), lambda i, j: (i, 0))]
    args = [h]
    for off in col_offsets:
        assert off % tn == 0
        in_specs.append(pl.BlockSpec((None, k_dim, tn), lambda i, j, o=off // tn: (layer, 0, j + o)))
        args.append(w)
    if has_bias:
        b2 = bias.reshape(1, -1)
        for off in col_offsets:
            in_specs.append(pl.BlockSpec((1, tn), lambda i, j, o=off // tn: (0, j + o)))
            args.append(b2)
    if epilogue == "residual":
        in_specs += [pl.BlockSpec((tm, tn), lambda i, j: (i, j)),
                     _mod_spec(gate, tm, seq_len, tn, lambda j: j)]
        args += [x, gate]
    out_specs = [pl.BlockSpec((tm, tn), lambda i, j: (i, j)) for _ in out_dtypes]
    out_shape = [jax.ShapeDtypeStruct((t, n_out), dt) for dt in out_dtypes]
    res = pl.pallas_call(
        functools.partial(_mm_kernel, n_w=n_w, has_bias=has_bias, epilogue=epilogue),
        grid=(t // tm, n_out // tn), in_specs=in_specs, out_specs=out_specs, out_shape=out_shape,
        compiler_params=_cp("arbitrary", "arbitrary"), name="mm_" + epilogue,
    )(*args)
    return res if len(res) > 1 else res[0]


def _sgu_mix_kernel(v_ref, u_ref, lg_ref, lb_ref, ws_ref, mask_ref, bst_ref, o_ref, *vout,
                    groups):
    v = v_ref[...]
    mu = jnp.mean(v, axis=-1, keepdims=True)
    vc = v - mu
    var = jnp.mean(vc * vc, axis=-1, keepdims=True)
    vn = vc * lax.rsqrt(var + EPS) * lg_ref[...] + lb_ref[...]
    if vout:
        vout[0][...] = vn
    eg = v.shape[1] // groups
    for g in range(groups):
        w = (ws_ref[g] * mask_ref[...]).astype(BF16)
        vg = vn[:, g * eg:(g + 1) * eg].astype(BF16)
        mixed = jnp.dot(w, vg, preferred_element_type=F32) + bst_ref[:, g:g + 1]
        u = u_ref[:, g * eg:(g + 1) * eg].astype(F32)
        o_ref[:, g * eg:(g + 1) * eg] = (u * mixed).astype(BF16)


def sgu_mix(v, u, ln_g, ln_b, ws, mask, bst, *, emit_v):
    t, e = v.shape
    n = ws.shape[1]
    groups = ws.shape[0]
    row = lambda i: (i, 0)
    full2 = lambda i: (0, 0)
    out_specs = [pl.BlockSpec((n, e), row)]
    out_shape = [jax.ShapeDtypeStruct((t, e), BF16)]
    if emit_v:
        out_specs.append(pl.BlockSpec((n, e), row))
        out_shape.append(jax.ShapeDtypeStruct((t, e), F32))
    return pl.pallas_call(
        functools.partial(_sgu_mix_kernel, groups=groups),
        grid=(t // n,),
        in_specs=[pl.BlockSpec((n, e), row), pl.BlockSpec((n, e), row),
                  pl.BlockSpec((1, e), full2), pl.BlockSpec((1, e), full2),
                  pl.BlockSpec((groups, n, n), lambda i: (0, 0, 0)),
                  pl.BlockSpec((n, n), full2), pl.BlockSpec((n, groups), full2)],
        out_specs=out_specs, out_shape=out_shape,
        compiler_params=_cp("arbitrary"), name="sgu_mix",
    )(v, u, ln_g.reshape(1, e), ln_b.reshape(1, e), ws, mask, bst)


HALO = 32
CONV_COLS = 128


def _conv_kernel(*refs, tt, width, single):
    if single:
        cur_ref, hist_ref, dw_ref, dwb_ref, lg_ref, lb_ref, o_ref, xp_ref, y_ref, sh_ref = refs
    else:
        cur_ref, prev_ref, hist_ref, dw_ref, dwb_ref, lg_ref, lb_ref, o_ref, xp_ref, y_ref, sh_ref = refs
    d = cur_ref.shape[1]
    xp_ref[HALO:HALO + tt, :] = cur_ref[...]
    if single:
        xp_ref[0:HALO, :] = hist_ref[0]
    else:
        first = pl.program_id(1) == 0

        @pl.when(first)
        def _():
            xp_ref[0:HALO, :] = hist_ref[0]

        @pl.when(jnp.logical_not(first))
        def _():
            xp_ref[0:HALO, :] = prev_ref[...]

    base = HALO - (width - 1)
    n_b = min(SUBLANES, width)
    n_a = [(width - 1 - b) // SUBLANES + 1 for b in range(n_b)]
    for b in range(n_b):
        rows = tt + SUBLANES * (n_a[b] - 1)
        sh_ref[b, 0:rows, :] = xp_ref[base + b:base + b + rows, :]
    cw = min(CONV_COLS, d)
    for c in range(d // cw):
        cols = slice(c * cw, (c + 1) * cw)
        acc = jnp.zeros((tt, cw), F32)
        for b in range(n_b):
            for a in range(n_a[b]):
                j = SUBLANES * a + b
                acc = acc + dw_ref[j:j + 1, cols] * sh_ref[b, SUBLANES * a:SUBLANES * a + tt, cols]
        y_ref[:, cols] = acc + dwb_ref[:, cols]
    y = y_ref[...]
    mu = jnp.mean(y, axis=-1, keepdims=True)
    yc = y - mu
    var = jnp.mean(yc * yc, axis=-1, keepdims=True)
    yn = yc * lax.rsqrt(var + EPS) * lg_ref[...] + lb_ref[...]
    o_ref[...] = (yn * jax.nn.sigmoid(yn)).astype(BF16)


def conv_core(glu, hist, dw, dw_b, ln_g, ln_b, *, seq_len):
    t, d = glu.shape
    b = t // seq_len
    width = dw.shape[0]
    tt = _tile(seq_len, 128, HALO)
    single = tt == seq_len
    nb = seq_len // tt
    vec = lambda a: a.reshape(1, d)
    full2 = lambda bi, i: (0, 0)
    in_specs = [pl.BlockSpec((tt, d), lambda bi, i: (bi * nb + i, 0))]
    args = [glu]
    if not single:
        per = tt // HALO
        in_specs.append(pl.BlockSpec((HALO, d), lambda bi, i: (jnp.maximum((bi * nb + i) * per - 1, 0), 0)))
        args.append(glu)
    in_specs += [pl.BlockSpec((1, HALO, d), lambda bi, i: (bi, 0, 0)),
                 pl.BlockSpec((width, d), full2),
                 pl.BlockSpec((1, d), full2), pl.BlockSpec((1, d), full2), pl.BlockSpec((1, d), full2)]
    args += [hist, dw, vec(dw_b), vec(ln_g), vec(ln_b)]
    return pl.pallas_call(
        functools.partial(_conv_kernel, tt=tt, width=width, single=single),
        grid=(b, nb), in_specs=in_specs,
        out_specs=pl.BlockSpec((tt, d), lambda bi, i: (bi * nb + i, 0)),
        out_shape=jax.ShapeDtypeStruct((t, d), BF16),
        scratch_shapes=[pltpu.VMEM((HALO + tt, d), F32), pltpu.VMEM((tt, d), F32),
                        pltpu.VMEM((min(SUBLANES, width), tt + HALO, d), F32)],
        compiler_params=_cp("arbitrary", "arbitrary"), name="conv_core",
    )(*args)


def _attn_kernel(sink_ref, q_ref, *rest, n_pieces, kv_heads, q_per_kv, head_dim, window, n_heads,
                 mask_front):
    k_refs = rest[:n_pieces]
    v_refs = rest[n_pieces:2 * n_pieces]
    o_ref, k_scr, v_scr = rest[2 * n_pieces:]
    off = 0
    for kr, vr in zip(k_refs, v_refs):
        rows = kr.shape[-2]
        k_scr[off:off + rows, :] = kr[...].reshape(kr.shape[-2:]).astype(BF16)
        v_scr[off:off + rows, :] = vr[...].reshape(vr.shape[-2:]).astype(BF16)
        off += rows
    nq = q_ref.shape[0]
    nk = k_scr.shape[0]
    qi = lax.broadcasted_iota(jnp.int32, (nq, nk), 0)
    kj = lax.broadcasted_iota(jnp.int32, (nq, nk), 1)
    dist = jnp.abs(window + qi - kj).astype(F32)
    if mask_front:
        valid = kj >= window - pl.program_id(1) * nq
    scale = head_dim ** -0.5
    groups = [range(g * q_per_kv, (g + 1) * q_per_kv) for g in range(kv_heads)]
    scores, sinks = [], []
    for g, heads in enumerate(groups):
        q = jnp.concatenate([q_ref[:, hd * head_dim:(hd + 1) * head_dim] for hd in heads], axis=0)
        k = k_scr[:, g * head_dim:(g + 1) * head_dim]
        s = lax.dot_general(q.astype(BF16), k, (((1,), (1,)), ((), ())), preferred_element_type=F32)
        bias = []
        for hd in heads:
            b = (2.0 ** (-8.0 * (hd + 1) / n_heads)) * dist
            bias.append(jnp.where(valid, b, -NEG_INF) if mask_front else b)
        scores.append(s * scale - jnp.concatenate(bias, axis=0))
        sinks.append(jnp.concatenate([jnp.full((nq, 1), sink_ref[hd], F32) for hd in heads], axis=0))
    maxes = [jnp.maximum(jnp.max(s, axis=-1, keepdims=True), sk) for s, sk in zip(scores, sinks)]
    probs = [jnp.exp(s - m) for s, m in zip(scores, maxes)]
    dens = [jnp.sum(p, axis=-1, keepdims=True) + jnp.exp(sk - m) for p, sk, m in zip(probs, sinks, maxes)]
    for g, heads in enumerate(groups):
        v = v_scr[:, g * head_dim:(g + 1) * head_dim]
        o = jnp.dot((probs[g] / dens[g]).astype(BF16), v, preferred_element_type=F32).astype(BF16)
        for r, hd in enumerate(heads):
            o_ref[:, hd * head_dim:(hd + 1) * head_dim] = o[r * nq:(r + 1) * nq]


def attention(qkv, k_pieces, v_pieces, sinks, *, grid, nq, q_dim, head_dim, window, mask_front):
    t = qkv.shape[0]
    nc = grid[1]
    kv_dim = k_pieces[0][1].block_shape[-1]
    band = sum(spec.block_shape[-2] for _, spec in k_pieces)
    n_heads = q_dim // head_dim
    kv_heads = kv_dim // head_dim
    kern = functools.partial(_attn_kernel, n_pieces=len(k_pieces), kv_heads=kv_heads,
                             q_per_kv=n_heads // kv_heads, head_dim=head_dim, window=window,
                             n_heads=n_heads, mask_front=mask_front)
    pieces = k_pieces + v_pieces
    return pl.pallas_call(
        kern, grid=grid,
        in_specs=[pl.BlockSpec(memory_space=pltpu.SMEM),
                  pl.BlockSpec((nq, q_dim), lambda bi, c: (bi * nc + c, 0))] + [sp for _, sp in pieces],
        out_specs=pl.BlockSpec((nq, q_dim), lambda bi, c: (bi * nc + c, 0)),
        out_shape=jax.ShapeDtypeStruct((t, q_dim), BF16),
        scratch_shapes=[pltpu.VMEM((band, kv_dim), BF16), pltpu.VMEM((band, kv_dim), BF16)],
        compiler_params=_cp("arbitrary", "arbitrary"), name="attention",
    )(sinks, qkv, *[a for a, _ in pieces])


def _topk_tables(s1, s2, a_scr, b_scr, c_scr, u, *, topk, n_cand):
    w = s1
    for r in range(topk):
        m = jnp.max(w, axis=0, keepdims=True)
        a_scr[u, r:r + 1, :] = m
        w = jnp.where(w == m, NEG_INF, w)
    w = s2
    rank2 = jnp.full(s2.shape, topk + 1.0, F32)
    for r in range(topk):
        m = jnp.max(w, axis=0, keepdims=True)
        b_scr[u, r:r + 1, :] = m
        hit = w == m
        w = jnp.where(hit, NEG_INF, w)
        rank2 = jnp.where(hit, r + 1.0, rank2)
    k = 0
    for r in range(topk):
        for j in range(topk // (r + 1)):
            c_scr[u, k:k + 1, :] = a_scr[u, r:r + 1, :] + b_scr[u, j:j + 1, :]
            k += 1
    rows = c_scr.shape[1]
    c_scr[u, n_cand:rows, :] = jnp.full((rows - n_cand, c_scr.shape[2]), NEG_INF, F32)
    cand = c_scr[u]
    ridx = lax.broadcasted_iota(jnp.int32, cand.shape, 0)
    m0 = jnp.max(cand, axis=0, keepdims=True)
    z = jnp.zeros_like(m0)
    m = m0
    for r in range(topk):
        m = jnp.max(cand, axis=0, keepdims=True)
        z = z + jnp.exp(m - m0)
        first = jnp.min(jnp.where(cand == m, ridx, rows), axis=0, keepdims=True)
        cand = jnp.where(ridx == first, NEG_INF, cand)
    tau = m
    a = a_scr[u]
    cnt = jnp.zeros_like(s1)
    for j in range(topk):
        ok = a + b_scr[u, j:j + 1, :] >= tau
        theta = jnp.min(jnp.where(ok, a, -NEG_INF), axis=0, keepdims=True)
        cnt = jnp.where(s1 >= theta, j + 1.0, cnt)
    e1 = jnp.exp(s1 - a_scr[u, 0:1, :])
    e2 = jnp.exp(s2 - b_scr[u, 0:1, :]) / z
    return cnt, e1, rank2, e2


def _peer_route_kernel(ht_ref, wqt_ref, sk_ref, cnt_ref, e1_ref, r2_ref, e2_ref, a_scr, b_scr, c_scr,
                       *, topk, n_cand, hps):
    hk = sk_ref.shape[1]
    nk = hk // 2
    nl = ht_ref.shape[1] // LANE
    def head_scores(hh):
        qt = jnp.dot(wqt_ref[hh * hk:(hh + 1) * hk, :], ht_ref[...], preferred_element_type=F32)
        return jnp.dot(sk_ref[hh], qt.astype(BF16), preferred_element_type=F32)

    scores = [head_scores(0)]
    for hh in range(hps):
        if hh + 1 < hps:
            scores.append(head_scores(hh + 1))
        for l in range(nl):
            lanes = slice(l * LANE, (l + 1) * LANE)
            cnt, e1, rank2, e2 = _topk_tables(scores[hh][0:nk, lanes], scores[hh][nk:hk, lanes],
                                              a_scr, b_scr, c_scr, hh * nl + l, topk=topk, n_cand=n_cand)
            cnt_ref[hh, l] = cnt
            e1_ref[hh, l] = e1
            r2_ref[hh, l] = rank2.astype(r2_ref.dtype)
            e2_ref[hh, l] = e2.astype(e2_ref.dtype)


def peer_route(ht, wqt, skbd, layer):
    d, t = ht.shape
    heads, hk = skbd.shape[1:3]
    nk = hk // 2
    tt = _tile(t, 256)
    nl = tt // LANE
    hps = 4
    n_cand = sum(PEER_TOPK // (r + 1) for r in range(PEER_TOPK))
    cand_rows = -(-n_cand // 8) * 8
    units = hps * nl
    out32 = jax.ShapeDtypeStruct((heads, t // LANE, nk, LANE), F32)
    out16 = jax.ShapeDtypeStruct((heads, t // LANE, nk, LANE), BF16)
    spec = pl.BlockSpec((hps, nl, nk, LANE), lambda i, h: (h, i, 0, 0))
    return pl.pallas_call(
        functools.partial(_peer_route_kernel, topk=PEER_TOPK, n_cand=n_cand, hps=hps),
        grid=(t // tt, heads // hps),
        in_specs=[pl.BlockSpec((d, tt), lambda i, h: (0, i)),
                  pl.BlockSpec((None, hps * hk, d), lambda i, h: (layer, h, 0)),
                  pl.BlockSpec((None, hps, hk, hk), lambda i, h: (layer, h, 0, 0))],
        out_specs=[spec, spec, spec, spec], out_shape=[out32, out32, out16, out16],
        scratch_shapes=[pltpu.VMEM((units, PEER_TOPK, LANE), F32), pltpu.VMEM((units, PEER_TOPK, LANE), F32),
                        pltpu.VMEM((units, cand_rows, LANE), F32)],
        compiler_params=_cp("arbitrary", "arbitrary"), name="peer_route",
    )(ht, wqt, skbd)


def _peer_dense_kernel(ht_ref, u_ref, vt_ref, cnt_ref, e1_ref, r2_ref, e2_ref, o_ref, a_scr, w_scr,
                       *, heads, n_i1, nk, tb, tc):
    @pl.when(pl.program_id(1) == 0)
    def _():
        o_ref[...] = jnp.zeros_like(o_ref)

    def row_bf16(ref, h, l, il):
        return jnp.broadcast_to(ref[h, l, il:il + 1, :], (nk, LANE)).astype(BF16)

    lpc = tc // LANE
    ec, d = u_ref.shape
    for c in range(tb // tc):
        cols = slice(c * tc, (c + 1) * tc)
        for half in (slice(0, ec // 2), slice(ec // 2, ec)):
            a_scr[c, half, :] = jnp.dot(u_ref[half, :], ht_ref[:, cols], preferred_element_type=F32)
    for c in range(tb // tc):
        cols = slice(c * tc, (c + 1) * tc)
        for il in range(n_i1):
            rows = slice(il * nk, (il + 1) * nk)
            parts = []
            for l in range(c * lpc, (c + 1) * lpc):
                gate = None
                for h in range(heads):
                    term = jnp.where(row_bf16(cnt_ref, h, l, il) >= r2_ref[h, l],
                                     row_bf16(e1_ref, h, l, il), 0.0) * e2_ref[h, l]
                    gate = term if gate is None else gate + term
                sub = slice((l - c * lpc) * LANE, (l - c * lpc + 1) * LANE)
                parts.append(_gelu(a_scr[c, rows, sub]).astype(BF16) * gate)
            w_scr[c, rows, :] = jnp.concatenate(parts, axis=1)
        for half in (slice(0, d // 2), slice(d // 2, d)):
            o_ref[half, cols] += jnp.dot(vt_ref[half, :], w_scr[c], preferred_element_type=F32)


def peer_dense(ht, u_bf, vt_bf, layer, cnt, e1, r2, e2):
    d, t = ht.shape
    n_exp = u_bf.shape[1]
    heads, _, nk, _ = cnt.shape
    tb = _tile(t, 1024)
    tc = _tile(tb, 256)
    nl = tb // LANE
    n_i1 = 8
    ec = n_i1 * nk
    once = pl.Buffered(1)
    return pl.pallas_call(
        functools.partial(_peer_dense_kernel, heads=heads, n_i1=n_i1, nk=nk, tb=tb, tc=tc),
        grid=(t // tb, n_exp // ec),
        in_specs=[pl.BlockSpec((d, tb), lambda i, j: (0, i), pipeline_mode=once),
                  pl.BlockSpec((None, ec, d), lambda i, j: (layer, j, 0)),
                  pl.BlockSpec((None, d, ec), lambda i, j: (layer, 0, j)),
                  pl.BlockSpec((heads, nl, n_i1, LANE), lambda i, j: (0, i, j, 0)),
                  pl.BlockSpec((heads, nl, n_i1, LANE), lambda i, j: (0, i, j, 0)),
                  pl.BlockSpec((heads, nl, nk, LANE), lambda i, j: (0, i, 0, 0), pipeline_mode=once),
                  pl.BlockSpec((heads, nl, nk, LANE), lambda i, j: (0, i, 0, 0), pipeline_mode=once)],
        out_specs=pl.BlockSpec((d, tb), lambda i, j: (0, i)),
        out_shape=jax.ShapeDtypeStruct((d, t), F32),
        scratch_shapes=[pltpu.VMEM((tb // tc, ec, tc), F32), pltpu.VMEM((tb // tc, ec, tc), BF16)],
        compiler_params=_cp("arbitrary", "arbitrary"), name="peer_dense",
    )(ht, u_bf, vt_bf, cnt, e1, r2, e2)


def _peer_layer(x, g, sc, sh, pw, layer, *, seq_len):
    ht = norm(x, g, sc, sh, seq_len=seq_len, transposed=True)
    cnt, e1, r2, e2 = peer_route(ht, pw["wqt"], pw["skbd"], layer)
    return peer_dense(ht, pw["u"], pw["vt"], layer, cnt, e1, r2, e2)


def _sgu_layer(x, h, gate, w, *, seq_len, emit_v):
    e = w["w_out"][0].shape[1]
    u, v = matmul(h, w["w_in"], w["b_in"], "sgu_in", col_offsets=(0, e), n_out=e,
                  out_dtypes=(BF16, F32), tm_pref=1024, tn_pref=512)
    res = sgu_mix(v, u, w["ln_g"], w["ln_b"], w["ws"], w["mask"], w["bst"], emit_v=emit_v)
    gated = res[0]
    xn = matmul(gated, w["w_out"], None, "residual", n_out=x.shape[1], out_dtypes=(F32,),
                tm_pref=512, tn_pref=512, x=x, gate=gate, seq_len=seq_len)
    return xn, (res[1] if emit_v else None)


def _conv_layer(x, h, gate, w, hist, *, seq_len, mod_seq):
    d = x.shape[1]
    glu = matmul(h, w["w_in"], w["b_in"], "glu", col_offsets=(0, d), n_out=d, out_dtypes=(F32,),
                 tm_pref=1024, tn_pref=512)
    act = conv_core(glu, hist, w["dw"], w["dw_b"], w["ln_g"], w["ln_b"], seq_len=seq_len)
    xn = matmul(act, w["w_out"], w["b_out"], "residual", n_out=d, out_dtypes=(F32,),
                tm_pref=1024, tn_pref=512, x=x, gate=gate, seq_len=mod_seq)
    return xn, glu


def _attn_layer(x, h, gate, w, caches, *, batch, seq_len, mod_seq, q_dim, head_dim, window):
    d = x.shape[1]
    n_qkv = w["w_qkv"][0].shape[2]
    qkv = matmul(h, w["w_qkv"], w["b_qkv"], "plain", n_out=n_qkv, out_dtypes=(F32,),
                 tm_pref=1024, tn_pref=512)
    kv_dim = (n_qkv - q_dim) // 2
    assert q_dim % kv_dim == 0
    col = {"k": q_dim // kv_dim, "v": q_dim // kv_dim + 1}
    if caches is None:
        nq = CHUNK
        nc = seq_len // nq
        back = window // nq

        def pieces(which):
            return [(qkv, pl.BlockSpec((nq, kv_dim), lambda bi, c, j=j, cb=col[which]:
                                       (bi * nc + jnp.maximum(c - back + j, 0), cb)))
                    for j in range(back + 1)]
        k_pieces, v_pieces = pieces("k"), pieces("v")
    else:
        nq, nc = seq_len, 1

        def pieces(which, cache):
            return [(cache, pl.BlockSpec((1, window, kv_dim), lambda bi, c: (bi, 0, 0))),
                    (qkv, pl.BlockSpec((nq, kv_dim), lambda bi, c, cb=col[which]: (bi, cb)))]
        k_pieces, v_pieces = pieces("k", caches[0]), pieces("v", caches[1])
    o = attention(qkv, k_pieces, v_pieces, w["sinks"], grid=(batch, nc), nq=nq, q_dim=q_dim,
                  head_dim=head_dim, window=window, mask_front=caches is None)
    xn = matmul(o, w["w_o"], None, "residual", n_out=d, out_dtypes=(F32,),
                tm_pref=1024, tn_pref=512, x=x, gate=gate, seq_len=mod_seq)
    k_new = qkv[:, q_dim:q_dim + kv_dim].reshape(batch, seq_len, kv_dim)
    v_new = qkv[:, q_dim + kv_dim:].reshape(batch, seq_len, kv_dim)
    if caches is not None:
        k_new = jnp.concatenate([caches[0], k_new], axis=1)
        v_new = jnp.concatenate([caches[1], v_new], axis=1)
    return xn, k_new[:, -window:], v_new[:, -window:]


def kernel(x_prompt, x_sample, cache_k_win, cache_v_win, state_conv, c_prompt, c_sample, norm_mix_g, norm_ch_g, norm_final_g, ada_w, ada_b, sgu_w_in, sgu_b_in, sgu_ln_g, sgu_ln_b, sgu_w_s, sgu_b_s, sgu_w_out, conv_w_in, conv_b_in, conv_dw, conv_dw_b, conv_ln_g, conv_ln_b, conv_w_out, conv_b_out, attn_w_qkv, attn_b_qkv, attn_sinks, attn_w_o, peer_w_q, peer_subkeys, peer_u, peer_v):
    bp, sp, d = x_prompt.shape
    bs, ss, _ = x_sample.shape
    depth = ada_w.shape[0]
    window, kv_heads, head_dim = cache_k_win.shape[2:]
    kv_dim = kv_heads * head_dim
    q_dim = attn_w_qkv.shape[2] - 2 * kv_dim
    conv_w = conv_dw.shape[1]
    peer_heads = peer_subkeys.shape[1]
    tp, ts = bp * sp, bs * ss
    assert ts == SGU_CHUNK and sp % SGU_CHUNK == 0 and sp % CHUNK == 0 and window % CHUNK == 0

    n_c = bp + bs
    rows = -(-n_c // 8) * 8
    c_all = jnp.concatenate([c_prompt, c_sample, jnp.zeros((rows - n_c, d), F32)], axis=0)
    mods = ada_all(c_all, ada_w, ada_b)

    def mod_vectors(layer):
        m = mods[layer].reshape(rows, 6, d)
        prompt = [m[:bp, k][:, None, :] for k in range(6)]
        sample = [jnp.repeat(m[bp:n_c, k], ss, axis=0)[None] for k in range(6)]
        return prompt, sample

    t_idx = jnp.arange(SGU_CHUNK)
    mask_p = ((t_idx[None, :] // CHUNK) <= (t_idx[:, None] // CHUNK)).astype(F32)
    blk = t_idx // ss
    mask_s = ((blk[None, :] == blk[:, None])
              & ((t_idx[None, :] % ss) // CHUNK <= (t_idx[:, None] % ss) // CHUNK)).astype(F32)

    sgu_w_in_bf, sgu_w_out_bf = sgu_w_in.astype(BF16), sgu_w_out.astype(BF16)
    conv_w_in_bf, conv_w_out_bf = conv_w_in.astype(BF16), conv_w_out.astype(BF16)
    attn_w_qkv_bf, attn_w_o_bf = attn_w_qkv.astype(BF16), attn_w_o.astype(BF16)
    nk, half = peer_subkeys.shape[3:]
    zk = jnp.zeros((depth, peer_heads, nk, half), F32)
    skbd = jnp.concatenate([jnp.concatenate([peer_subkeys[:, :, 0], zk], axis=3),
                            jnp.concatenate([zk, peer_subkeys[:, :, 1]], axis=3)], axis=2).astype(BF16)
    pw = dict(wqt=jnp.swapaxes(peer_w_q, 1, 2).astype(BF16), skbd=skbd,
              u=peer_u.astype(BF16), vt=jnp.swapaxes(peer_v, 1, 2).astype(BF16))

    xp = x_prompt.reshape(tp, d)
    xs = x_sample.reshape(ts, d)
    pend_p = pend_s = None
    ia = ib = ic = 0
    conv_p, conv_s, kwin_p, vwin_p, kwin_s, vwin_s, sgu_s = [], [], [], [], [], [], []
    for layer in range(depth):
        mp, ms = mod_vectors(layer)
        hp = norm(xp, norm_mix_g[layer], mp[1], mp[0], seq_len=sp, pending=pend_p, emit_x=pend_p is not None)
        hs = norm(xs, norm_mix_g[layer], ms[1], ms[0], seq_len=ts, pending=pend_s, emit_x=pend_s is not None)
        if pend_p is not None:
            (hp, xp), (hs, xs) = hp, hs
        kind = layer % N_MIXERS
        if kind == 0:
            w = dict(w_in=(sgu_w_in_bf, ia), b_in=sgu_b_in[ia], ln_g=sgu_ln_g[ia],
                     ln_b=sgu_ln_b[ia], w_out=(sgu_w_out_bf, ia))
            ws = sgu_w_s[ia]
            wp = dict(w, ws=ws, mask=mask_p, bst=sgu_b_s[ia].T)
            reps = SGU_CHUNK // ss
            wsm = dict(w, ws=jnp.tile(ws[:, :ss, :ss], (1, reps, reps)), mask=mask_s,
                       bst=jnp.tile(sgu_b_s[ia][:, :ss], (1, reps)).T)
            xp, _ = _sgu_layer(xp, hp, mp[2], wp, seq_len=sp, emit_v=False)
            xs, v_rows = _sgu_layer(xs, hs, ms[2], wsm, seq_len=ts, emit_v=True)
            sgu_s.append(v_rows.reshape(bs, ss, -1))
            ia += 1
        elif kind == 1:
            dw = conv_dw[ib]
            w = dict(w_in=(conv_w_in_bf, ib), b_in=conv_b_in[ib], dw=dw, dw_b=conv_dw_b[ib],
                     ln_g=conv_ln_g[ib], ln_b=conv_ln_b[ib], w_out=(conv_w_out_bf, ib),
                     b_out=conv_b_out[ib])
            pad = HALO - (conv_w - 1)
            hist_p = jnp.zeros((bp, HALO, d), F32)
            hist_s = jnp.pad(state_conv[ib], ((0, 0), (pad, 0), (0, 0)))
            xp, glu_p = _conv_layer(xp, hp, mp[2], w, hist_p, seq_len=sp, mod_seq=sp)
            xs, glu_s = _conv_layer(xs, hs, ms[2], w, hist_s, seq_len=ss, mod_seq=ts)
            conv_p.append(glu_p.reshape(bp, sp, d)[:, sp - (conv_w - 1):])
            conv_s.append(jnp.concatenate([state_conv[ib], glu_s.reshape(bs, ss, d)], axis=1)[:, -(conv_w - 1):])
            ib += 1
        else:
            w = dict(w_qkv=(attn_w_qkv_bf, ic), b_qkv=attn_b_qkv[ic], sinks=attn_sinks[ic],
                     w_o=(attn_w_o_bf, ic))
            caches = (cache_k_win[ic].reshape(bs, window, kv_dim), cache_v_win[ic].reshape(bs, window, kv_dim))
            common = dict(q_dim=q_dim, head_dim=head_dim, window=window)
            xp, kp, vp = _attn_layer(xp, hp, mp[2], w, None, batch=bp, seq_len=sp, mod_seq=sp, **common)
            xs, ks, vs = _attn_layer(xs, hs, ms[2], w, caches, batch=bs, seq_len=ss, mod_seq=ts, **common)
            kwin_p.append(kp.reshape(bp, window, kv_heads, head_dim))
            vwin_p.append(vp.reshape(bp, window, kv_heads, head_dim))
            kwin_s.append(ks.reshape(bs, window, kv_heads, head_dim))
            vwin_s.append(vs.reshape(bs, window, kv_heads, head_dim))
            ic += 1

        pend_p = (_peer_layer(xp, norm_ch_g[layer], mp[4], mp[3], pw, layer, seq_len=sp), mp[5])
        pend_s = (_peer_layer(xs, norm_ch_g[layer], ms[4], ms[3], pw, layer, seq_len=ts), ms[5])

    y_prompt = norm(xp, norm_final_g, seq_len=sp, out_dtype=F32, pending=pend_p).reshape(bp, sp, d)
    y_sample = norm(xs, norm_final_g, seq_len=ts, out_dtype=F32, pending=pend_s).reshape(bs, ss, d)
    return (y_prompt, y_sample, jnp.stack(conv_p), jnp.stack(kwin_p), jnp.stack(vwin_p),
            jnp.stack(conv_s), jnp.stack(kwin_s), jnp.stack(vwin_s), jnp.stack(sgu_s))
```

```python
import functools

import jax
import jax.numpy as jnp
from jax import lax
from jax.experimental import pallas as pl
from jax.experimental.pallas import tpu as pltpu

F32 = jnp.float32
BF16 = jnp.bfloat16

EPS = 1e-6
CHUNK = 64
SGU_CHUNK = 128
SGU_GROUPS = 8
PEER_TOPK = 16
NEG_INF = -1e30
N_MIXERS = 3

V7X_VMEM_BYTES = 64 * 1024 * 1024
VMEM_LIMIT = V7X_VMEM_BYTES - 8 * 1024 * 1024
LANE = 128
SUBLANES = 8


def _cp(*sem):
    return pltpu.CompilerParams(dimension_semantics=sem, vmem_limit_bytes=VMEM_LIMIT)


def _tile(n, pref, mult=LANE):
    if n <= pref:
        return n
    t = (pref // mult) * mult
    while t >= mult:
        if n % t == 0:
            return t
        t -= mult
    return n


def _gelu(x):
    return jax.nn.gelu(x, approximate=True)


def _ada_kernel(c_ref, w_ref, b_ref, o_ref):
    c = c_ref[...]
    a = (c * jax.nn.sigmoid(c)).astype(BF16)
    o_ref[0] = jnp.dot(a, w_ref[0].astype(BF16), preferred_element_type=F32) + b_ref[0]


def ada_all(c_all, ada_w, ada_b):
    n_layers, d, n = ada_w.shape
    rows = c_all.shape[0]
    tn = _tile(n, 1024)
    return pl.pallas_call(
        _ada_kernel,
        grid=(n_layers, n // tn),
        in_specs=[pl.BlockSpec((rows, d), lambda l, j: (0, 0)),
                  pl.BlockSpec((1, d, tn), lambda l, j: (l, 0, j)),
                  pl.BlockSpec((1, 1, tn), lambda l, j: (l, 0, j))],
        out_specs=pl.BlockSpec((1, rows, tn), lambda l, j: (l, 0, j)),
        out_shape=jax.ShapeDtypeStruct((n_layers, rows, n), F32),
        compiler_params=_cp("arbitrary", "arbitrary"),
        name="ada",
    )(c_all, ada_w, ada_b.reshape(n_layers, 1, n))


def _mod_spec(arr, tm, seq_len, tn=None, col_of=None):
    _, r, d = arr.shape
    if tn is None:
        return pl.BlockSpec((1, r, d), lambda i, *_: ((i * tm) // seq_len, 0, 0))
    return pl.BlockSpec((1, r, tn), lambda i, j: ((i * tm) // seq_len, 0, col_of(j)))


def _norm_kernel(*refs, modulated, transposed, pending, emit_x):
    refs = list(refs)
    x = refs.pop(0)[...]
    if pending:
        yt_ref, pg_ref = refs.pop(0), refs.pop(0)
        x = x + pg_ref[0] * yt_ref[...].T
    g_ref = refs.pop(0)
    y = x * lax.rsqrt(jnp.mean(x * x, axis=-1, keepdims=True) + EPS) * g_ref[...]
    if modulated:
        sc_ref, sh_ref = refs.pop(0), refs.pop(0)
        y = y * (1.0 + sc_ref[0]) + sh_ref[0]
    o_ref = refs.pop(0)
    if transposed:
        o_ref[...] = y.T.astype(o_ref.dtype)
    else:
        o_ref[...] = y.astype(o_ref.dtype)
    if emit_x:
        refs.pop(0)[...] = x


def norm(x, g, scale=None, shift=None, *, seq_len, transposed=False, out_dtype=BF16, pending=None,
         emit_x=False):
    t, d = x.shape
    tm = _tile(seq_len, 512)
    modulated = scale is not None
    row = pl.BlockSpec((tm, d), lambda i: (i, 0))
    in_specs, args = [row], [x]
    if pending is not None:
        in_specs += [pl.BlockSpec((d, tm), lambda i: (0, i)), _mod_spec(pending[1], tm, seq_len)]
        args += list(pending)
    in_specs.append(pl.BlockSpec((1, d), lambda i: (0, 0)))
    args.append(g.reshape(1, d))
    if modulated:
        in_specs += [_mod_spec(scale, tm, seq_len), _mod_spec(shift, tm, seq_len)]
        args += [scale, shift]
    if transposed:
        out_specs = [pl.BlockSpec((d, tm), lambda i: (0, i))]
        out_shape = [jax.ShapeDtypeStruct((d, t), out_dtype)]
    else:
        out_specs = [row]
        out_shape = [jax.ShapeDtypeStruct((t, d), out_dtype)]
    if emit_x:
        out_specs.append(row)
        out_shape.append(jax.ShapeDtypeStruct((t, d), F32))
    res = pl.pallas_call(
        functools.partial(_norm_kernel, modulated=modulated, transposed=transposed,
                          pending=pending is not None, emit_x=emit_x),
        grid=(t // tm,), in_specs=in_specs, out_specs=out_specs, out_shape=out_shape,
        compiler_params=_cp("arbitrary"), name="norm",
    )(*args)
    return res if emit_x else res[0]


def _mm_kernel(*refs, n_w, has_bias, epilogue):
    h_ref = refs[0]
    w_refs = refs[1:1 + n_w]
    pos = 1 + n_w
    b_refs = refs[pos:pos + n_w] if has_bias else ()
    pos += n_w if has_bias else 0
    h = h_ref[...]
    accs = []
    for k in range(n_w):
        acc = jnp.dot(h, w_refs[k][...], preferred_element_type=F32)
        if has_bias:
            acc = acc + b_refs[k][...]
        accs.append(acc)
    rest = refs[pos:]
    if epilogue == "sgu_in":
        rest[0][...] = _gelu(accs[0]).astype(BF16)
        rest[1][...] = _gelu(accs[1])
    elif epilogue == "glu":
        rest[0][...] = accs[0] * jax.nn.sigmoid(accs[1])
    elif epilogue == "plain":
        rest[0][...] = accs[0]
    elif epilogue == "residual":
        x_ref, gate_ref, o_ref = rest
        o_ref[...] = x_ref[...] + gate_ref[0] * accs[0]
    else:
        raise ValueError(epilogue)


def matmul(h, w, bias, epilogue, *, col_offsets=(0,), n_out, out_dtypes, tm_pref, tn_pref,
           x=None, gate=None, seq_len=None):
    w, layer = w
    t, k_dim = h.shape
    tm = _tile(t if seq_len is None else seq_len, tm_pref, 8)
    tn = _tile(n_out, tn_pref)
    n_w = len(col_offsets)
    has_bias = bias is not None
    in_specs = [pl.BlockSpec((tm, k_dim), lambda i, j: (i, 0))]
    args = [h]
    for off in col_offsets:
        assert off % tn == 0
        in_specs.append(pl.BlockSpec((None, k_dim, tn), lambda i, j, o=off // tn: (layer, 0, j + o)))
        args.append(w)
    if has_bias:
        b2 = bias.reshape(1, -1)
        for off in col_offsets:
            in_specs.append(pl.BlockSpec((1, tn), lambda i, j, o=off // tn: (0, j + o)))
            args.append(b2)
    if epilogue == "residual":
        in_specs += [pl.BlockSpec((tm, tn), lambda i, j: (i, j)),
                     _mod_spec(gate, tm, seq_len, tn, lambda j: j)]
        args += [x, gate]
    out_specs = [pl.BlockSpec((tm, tn), lambda i, j: (i, j)) for _ in out_dtypes]
    out_shape = [jax.ShapeDtypeStruct((t, n_out), dt) for dt in out_dtypes]
    res = pl.pallas_call(
        functools.partial(_mm_kernel, n_w=n_w, has_bias=has_bias, epilogue=epilogue),
        grid=(t // tm, n_out // tn), in_specs=in_specs, out_specs=out_specs, out_shape=out_shape,
        compiler_params=_cp("arbitrary", "arbitrary"), name="mm_" + epilogue,
    )(*args)
    return res if len(res) > 1 else res[0]


def _sgu_mix_kernel(v_ref, u_ref, lg_ref, lb_ref, ws_ref, mask_ref, bst_ref, o_ref, *vout,
                    groups):
    v = v_ref[...]
    mu = jnp.mean(v, axis=-1, keepdims=True)
    vc = v - mu
    var = jnp.mean(vc * vc, axis=-1, keepdims=True)
    vn = vc * lax.rsqrt(var + EPS) * lg_ref[...] + lb_ref[...]
    if vout:
        vout[0][...] = vn
    eg = v.shape[1] // groups
    for g in range(groups):
        w = (ws_ref[g] * mask_ref[...]).astype(BF16)
        vg = vn[:, g * eg:(g + 1) * eg].astype(BF16)
        mixed = jnp.dot(w, vg, preferred_element_type=F32) + bst_ref[:, g:g + 1]
        u = u_ref[:, g * eg:(g + 1) * eg].astype(F32)
        o_ref[:, g * eg:(g + 1) * eg] = (u * mixed).astype(BF16)


def sgu_mix(v, u, ln_g, ln_b, ws, mask, bst, *, emit_v):
    t, e = v.shape
    n = ws.shape[1]
    groups = ws.shape[0]
    row = lambda i: (i, 0)
    full2 = lambda i: (0, 0)
    out_specs = [pl.BlockSpec((n, e), row)]
    out_shape = [jax.ShapeDtypeStruct((t, e), BF16)]
    if emit_v:
        out_specs.append(pl.BlockSpec((n, e), row))
        out_shape.append(jax.ShapeDtypeStruct((t, e), F32))
    return pl.pallas_call(
        functools.partial(_sgu_mix_kernel, groups=groups),
        grid=(t // n,),
        in_specs=[pl.BlockSpec((n, e), row), pl.BlockSpec((n, e), row),
                  pl.BlockSpec((1, e), full2), pl.BlockSpec((1, e), full2),
                  pl.BlockSpec((groups, n, n), lambda i: (0, 0, 0)),
                  pl.BlockSpec((n, n), full2), pl.BlockSpec((n, groups), full2)],
        out_specs=out_specs, out_shape=out_shape,
        compiler_params=_cp("arbitrary"), name="sgu_mix",
    )(v, u, ln_g.reshape(1, e), ln_b.reshape(1, e), ws, mask, bst)


HALO = 32
CONV_COLS = 128


def _conv_kernel(*refs, tt, width, single):
    if single:
        cur_ref, hist_ref, dw_ref, dwb_ref, lg_ref, lb_ref, o_ref, xp_ref, y_ref, sh_ref = refs
    else:
        cur_ref, prev_ref, hist_ref, dw_ref, dwb_ref, lg_ref, lb_ref, o_ref, xp_ref, y_ref, sh_ref = refs
    d = cur_ref.shape[1]
    xp_ref[HALO:HALO + tt, :] = cur_ref[...]
    if single:
        xp_ref[0:HALO, :] = hist_ref[0]
    else:
        first = pl.program_id(1) == 0

        @pl.when(first)
        def _():
            xp_ref[0:HALO, :] = hist_ref[0]

        @pl.when(jnp.logical_not(first))
        def _():
            xp_ref[0:HALO, :] = prev_ref[...]

    base = HALO - (width - 1)
    n_b = min(SUBLANES, width)
    n_a = [(width - 1 - b) // SUBLANES + 1 for b in range(n_b)]
    for b in range(n_b):
        rows = tt + SUBLANES * (n_a[b] - 1)
        sh_ref[b, 0:rows, :] = xp_ref[base + b:base + b + rows, :]
    cw = min(CONV_COLS, d)
    for c in range(d // cw):
        cols = slice(c * cw, (c + 1) * cw)
        acc = jnp.zeros((tt, cw), F32)
        for b in range(n_b):
            for a in range(n_a[b]):
                j = SUBLANES * a + b
                acc = acc + dw_ref[j:j + 1, cols] * sh_ref[b, SUBLANES * a:SUBLANES * a + tt, cols]
        y_ref[:, cols] = acc + dwb_ref[:, cols]
    y = y_ref[...]
    mu = jnp.mean(y, axis=-1, keepdims=True)
    yc = y - mu
    var = jnp.mean(yc * yc, axis=-1, keepdims=True)
    yn = yc * lax.rsqrt(var + EPS) * lg_ref[...] + lb_ref[...]
    o_ref[...] = (yn * jax.nn.sigmoid(yn)).astype(BF16)


def conv_core(glu, hist, dw, dw_b, ln_g, ln_b, *, seq_len):
    t, d = glu.shape
    b = t // seq_len
    width = dw.shape[0]
    tt = _tile(seq_len, 128, HALO)
    single = tt == seq_len
    nb = seq_len // tt
    vec = lambda a: a.reshape(1, d)
    full2 = lambda bi, i: (0, 0)
    in_specs = [pl.BlockSpec((tt, d), lambda bi, i: (bi * nb + i, 0))]
    args = [glu]
    if not single:
        per = tt // HALO
        in_specs.append(pl.BlockSpec((HALO, d), lambda bi, i: (jnp.maximum((bi * nb + i) * per - 1, 0), 0)))
        args.append(glu)
    in_specs += [pl.BlockSpec((1, HALO, d), lambda bi, i: (bi, 0, 0)),
                 pl.BlockSpec((width, d), full2),
                 pl.BlockSpec((1, d), full2), pl.BlockSpec((1, d), full2), pl.BlockSpec((1, d), full2)]
    args += [hist, dw, vec(dw_b), vec(ln_g), vec(ln_b)]
    return pl.pallas_call(
        functools.partial(_conv_kernel, tt=tt, width=width, single=single),
        grid=(b, nb), in_specs=in_specs,
        out_specs=pl.BlockSpec((tt, d), lambda bi, i: (bi * nb + i, 0)),
        out_shape=jax.ShapeDtypeStruct((t, d), BF16),
        scratch_shapes=[pltpu.VMEM((HALO + tt, d), F32), pltpu.VMEM((tt, d), F32),
                        pltpu.VMEM((min(SUBLANES, width), tt + HALO, d), F32)],
        compiler_params=_cp("arbitrary", "arbitrary"), name="conv_core",
    )(*args)


def _attn_kernel(sink_ref, q_ref, *rest, n_pieces, kv_heads, q_per_kv, head_dim, window, n_heads,
                 mask_front):
    k_refs = rest[:n_pieces]
    v_refs = rest[n_pieces:2 * n_pieces]
    o_ref, k_scr, v_scr = rest[2 * n_pieces:]
    off = 0
    for kr, vr in zip(k_refs, v_refs):
        rows = kr.shape[-2]
        k_scr[off:off + rows, :] = kr[...].reshape(kr.shape[-2:]).astype(BF16)
        v_scr[off:off + rows, :] = vr[...].reshape(vr.shape[-2:]).astype(BF16)
        off += rows
    nq = q_ref.shape[0]
    nk = k_scr.shape[0]
    qi = lax.broadcasted_iota(jnp.int32, (nq, nk), 0)
    kj = lax.broadcasted_iota(jnp.int32, (nq, nk), 1)
    dist = jnp.abs(window + qi - kj).astype(F32)
    if mask_front:
        valid = kj >= window - pl.program_id(1) * nq
    scale = head_dim ** -0.5
    groups = [range(g * q_per_kv, (g + 1) * q_per_kv) for g in range(kv_heads)]
    scores, sinks = [], []
    for g, heads in enumerate(groups):
        q = jnp.concatenate([q_ref[:, hd * head_dim:(hd + 1) * head_dim] for hd in heads], axis=0)
        k = k_scr[:, g * head_dim:(g + 1) * head_dim]
        s = lax.dot_general(q.astype(BF16), k, (((1,), (1,)), ((), ())), preferred_element_type=F32)
        bias = []
        for hd in heads:
            b = (2.0 ** (-8.0 * (hd + 1) / n_heads)) * dist
            bias.append(jnp.where(valid, b, -NEG_INF) if mask_front else b)
        scores.append(s * scale - jnp.concatenate(bias, axis=0))
        sinks.append(jnp.concatenate([jnp.full((nq, 1), sink_ref[hd], F32) for hd in heads], axis=0))
    maxes = [jnp.maximum(jnp.max(s, axis=-1, keepdims=True), sk) for s, sk in zip(scores, sinks)]
    probs = [jnp.exp(s - m) for s, m in zip(scores, maxes)]
    dens = [jnp.sum(p, axis=-1, keepdims=True) + jnp.exp(sk - m) for p, sk, m in zip(probs, sinks, maxes)]
    for g, heads in enumerate(groups):
        v = v_scr[:, g * head_dim:(g + 1) * head_dim]
        o = jnp.dot((probs[g] / dens[g]).astype(BF16), v, preferred_element_type=F32).astype(BF16)
        for r, hd in enumerate(heads):
            o_ref[:, hd * head_dim:(hd + 1) * head_dim] = o[r * nq:(r + 1) * nq]


def attention(qkv, k_pieces, v_pieces, sinks, *, grid, nq, q_dim, head_dim, window, mask_front):
    t = qkv.shape[0]
    nc = grid[1]
    kv_dim = k_pieces[0][1].block_shape[-1]
    band = sum(spec.block_shape[-2] for _, spec in k_pieces)
    n_heads = q_dim // head_dim
    kv_heads = kv_dim // head_dim
    kern = functools.partial(_attn_kernel, n_pieces=len(k_pieces), kv_heads=kv_heads,
                             q_per_kv=n_heads // kv_heads, head_dim=head_dim, window=window,
                             n_heads=n_heads, mask_front=mask_front)
    pieces = k_pieces + v_pieces
    return pl.pallas_call(
        kern, grid=grid,
        in_specs=[pl.BlockSpec(memory_space=pltpu.SMEM),
                  pl.BlockSpec((nq, q_dim), lambda bi, c: (bi * nc + c, 0))] + [sp for _, sp in pieces],
        out_specs=pl.BlockSpec((nq, q_dim), lambda bi, c: (bi * nc + c, 0)),
        out_shape=jax.ShapeDtypeStruct((t, q_dim), BF16),
        scratch_shapes=[pltpu.VMEM((band, kv_dim), BF16), pltpu.VMEM((band, kv_dim), BF16)],
        compiler_params=_cp("arbitrary", "arbitrary"), name="attention",
    )(sinks, qkv, *[a for a, _ in pieces])


def _sort_network(n):
    pairs = []
    p = 1
    while p < n:
        k = p
        while k >= 1:
            for j in range(k % p, n - k, 2 * k):
                for i in range(min(k, n - j - k)):
                    if (i + j) // (2 * p) == (i + j + k) // (2 * p):
                        pairs.append((i + j, i + j + k))
            k //= 2
        p *= 2
    return pairs


def _top_values(s, scr, u, topk):
    n = s.shape[0] // SUBLANES
    assert n >= topk and n & (n - 1) == 0
    vs = [s[SUBLANES * g:SUBLANES * (g + 1), :] for g in range(n)]
    for i, j in _sort_network(n):
        vs[i], vs[j] = jnp.maximum(vs[i], vs[j]), jnp.minimum(vs[i], vs[j])
    for r in range(topk):
        m = jnp.max(vs[0], axis=0, keepdims=True)
        scr[u, r:r + 1, :] = m
        hit = vs[0] == m
        for k in range(topk - r - 1):
            vs[k] = jnp.where(hit, vs[k + 1], vs[k])


def _topk_tables(s1, s2, a_scr, b_scr, c_scr, u, *, topk, n_cand):
    _top_values(s1, a_scr, u, topk)
    _top_values(s2, b_scr, u, topk)
    rank2 = jnp.full(s2.shape, topk + 1.0, F32)
    for j in reversed(range(topk)):
        rank2 = jnp.where(s2 >= b_scr[u, j:j + 1, :], j + 1.0, rank2)
    k = 0
    for r in range(topk):
        for j in range(topk // (r + 1)):
            c_scr[u, k:k + 1, :] = a_scr[u, r:r + 1, :] + b_scr[u, j:j + 1, :]
            k += 1
    rows = c_scr.shape[1]
    c_scr[u, n_cand:rows, :] = jnp.full((rows - n_cand, c_scr.shape[2]), NEG_INF, F32)
    cand = c_scr[u]
    ridx = lax.broadcasted_iota(jnp.int32, cand.shape, 0)
    m0 = jnp.max(cand, axis=0, keepdims=True)
    z = jnp.zeros_like(m0)
    m = m0
    for r in range(topk):
        m = jnp.max(cand, axis=0, keepdims=True)
        z = z + jnp.exp(m - m0)
        first = jnp.min(jnp.where(cand == m, ridx, rows), axis=0, keepdims=True)
        cand = jnp.where(ridx == first, NEG_INF, cand)
    tau = m
    a = a_scr[u]
    cnt = jnp.zeros_like(s1)
    for j in range(topk):
        ok = a + b_scr[u, j:j + 1, :] >= tau
        theta = jnp.min(jnp.where(ok, a, -NEG_INF), axis=0, keepdims=True)
        cnt = jnp.where(s1 >= theta, j + 1.0, cnt)
    e1 = jnp.exp(s1 - a_scr[u, 0:1, :])
    e2 = jnp.exp(s2 - b_scr[u, 0:1, :]) / z
    return cnt, e1, rank2, e2


def _peer_route_kernel(ht_ref, wqt_ref, sk_ref, cnt_ref, e1_ref, r2_ref, e2_ref, a_scr, b_scr, c_scr,
                       *, topk, n_cand, hps):
    hk = sk_ref.shape[1]
    nk = hk // 2
    nl = ht_ref.shape[1] // LANE
    def head_scores(hh):
        qt = jnp.dot(wqt_ref[hh * hk:(hh + 1) * hk, :], ht_ref[...], preferred_element_type=F32)
        return jnp.dot(sk_ref[hh], qt.astype(BF16), preferred_element_type=F32)

    scores = [head_scores(0)]
    for hh in range(hps):
        if hh + 1 < hps:
            scores.append(head_scores(hh + 1))
        for l in range(nl):
            lanes = slice(l * LANE, (l + 1) * LANE)
            cnt, e1, rank2, e2 = _topk_tables(scores[hh][0:nk, lanes], scores[hh][nk:hk, lanes],
                                              a_scr, b_scr, c_scr, hh * nl + l, topk=topk, n_cand=n_cand)
            cnt_ref[hh, l] = cnt
            e1_ref[hh, l] = e1
            r2_ref[hh, l] = rank2.astype(r2_ref.dtype)
            e2_ref[hh, l] = e2.astype(e2_ref.dtype)


def peer_route(ht, wqt, skbd, layer):
    d, t = ht.shape
    heads, hk = skbd.shape[1:3]
    nk = hk // 2
    tt = _tile(t, 512)
    nl = tt // LANE
    hps = 2
    n_cand = sum(PEER_TOPK // (r + 1) for r in range(PEER_TOPK))
    cand_rows = -(-n_cand // 8) * 8
    units = hps * nl
    out32 = jax.ShapeDtypeStruct((heads, t // LANE, nk, LANE), F32)
    out16 = jax.ShapeDtypeStruct((heads, t // LANE, nk, LANE), BF16)
    spec = pl.BlockSpec((hps, nl, nk, LANE), lambda i, h: (h, i, 0, 0))
    return pl.pallas_call(
        functools.partial(_peer_route_kernel, topk=PEER_TOPK, n_cand=n_cand, hps=hps),
        grid=(t // tt, heads // hps),
        in_specs=[pl.BlockSpec((d, tt), lambda i, h: (0, i)),
                  pl.BlockSpec((None, hps * hk, d), lambda i, h: (layer, h, 0)),
                  pl.BlockSpec((None, hps, hk, hk), lambda i, h: (layer, h, 0, 0))],
        out_specs=[spec, spec, spec, spec], out_shape=[out32, out32, out16, out16],
        scratch_shapes=[pltpu.VMEM((units, PEER_TOPK, LANE), F32), pltpu.VMEM((units, PEER_TOPK, LANE), F32),
                        pltpu.VMEM((units, cand_rows, LANE), F32)],
        compiler_params=_cp("arbitrary", "arbitrary"), name="peer_route",
    )(ht, wqt, skbd)


def _peer_dense_kernel(ht_ref, u_ref, vt_ref, cnt_ref, e1_ref, r2_ref, e2_ref, o_ref, a_scr, w_scr,
                       *, heads, n_i1, nk, tb, tc):
    @pl.when(pl.program_id(1) == 0)
    def _():
        o_ref[...] = jnp.zeros_like(o_ref)

    def row_bf16(ref, h, l, il):
        return jnp.broadcast_to(ref[h, l, il:il + 1, :], (nk, LANE)).astype(BF16)

    lpc = tc // LANE
    ec, d = u_ref.shape
    def expert_inputs(c):
        cols = slice(c * tc, (c + 1) * tc)
        for half in (slice(0, ec // 2), slice(ec // 2, ec)):
            a_scr[c, half, :] = jnp.dot(u_ref[half, :], ht_ref[:, cols], preferred_element_type=F32)

    def gated_weights(c):
        for il in range(n_i1):
            rows = slice(il * nk, (il + 1) * nk)
            parts = []
            for l in range(c * lpc, (c + 1) * lpc):
                gate = None
                for h in range(heads):
                    term = jnp.where(row_bf16(cnt_ref, h, l, il) >= r2_ref[h, l],
                                     row_bf16(e1_ref, h, l, il), 0.0) * e2_ref[h, l]
                    gate = term if gate is None else gate + term
                sub = slice((l - c * lpc) * LANE, (l - c * lpc + 1) * LANE)
                parts.append(_gelu(a_scr[c, rows, sub]).astype(BF16) * gate)
            w_scr[c, rows, :] = jnp.concatenate(parts, axis=1)

    def expert_outputs(c):
        cols = slice(c * tc, (c + 1) * tc)
        for half in (slice(0, d // 2), slice(d // 2, d)):
            o_ref[half, cols] += jnp.dot(vt_ref[half, :], w_scr[c], preferred_element_type=F32)

    n_chains = tb // tc
    for c in range(n_chains):
        expert_inputs(c)
    for c in range(n_chains):
        gated_weights(c)
        expert_outputs(c)


def peer_dense(ht, u_bf, vt_bf, layer, cnt, e1, r2, e2):
    d, t = ht.shape
    n_exp = u_bf.shape[1]
    heads, _, nk, _ = cnt.shape
    tb = _tile(t, 1024)
    tc = _tile(tb, 256)
    nl = tb // LANE
    n_i1 = 8
    ec = n_i1 * nk
    once = pl.Buffered(1)
    return pl.pallas_call(
        functools.partial(_peer_dense_kernel, heads=heads, n_i1=n_i1, nk=nk, tb=tb, tc=tc),
        grid=(t // tb, n_exp // ec),
        in_specs=[pl.BlockSpec((d, tb), lambda i, j: (0, i), pipeline_mode=once),
                  pl.BlockSpec((None, ec, d), lambda i, j: (layer, j, 0)),
                  pl.BlockSpec((None, d, ec), lambda i, j: (layer, 0, j)),
                  pl.BlockSpec((heads, nl, n_i1, LANE), lambda i, j: (0, i, j, 0)),
                  pl.BlockSpec((heads, nl, n_i1, LANE), lambda i, j: (0, i, j, 0)),
                  pl.BlockSpec((heads, nl, nk, LANE), lambda i, j: (0, i, 0, 0), pipeline_mode=once),
                  pl.BlockSpec((heads, nl, nk, LANE), lambda i, j: (0, i, 0, 0), pipeline_mode=once)],
        out_specs=pl.BlockSpec((d, tb), lambda i, j: (0, i)),
        out_shape=jax.ShapeDtypeStruct((d, t), F32),
        scratch_shapes=[pltpu.VMEM((tb // tc, ec, tc), F32), pltpu.VMEM((tb // tc, ec, tc), BF16)],
        compiler_params=_cp("arbitrary", "arbitrary"), name="peer_dense",
    )(ht, u_bf, vt_bf, cnt, e1, r2, e2)


def _peer_layer(x, g, sc, sh, pw, layer, *, seq_len):
    ht = norm(x, g, sc, sh, seq_len=seq_len, transposed=True)
    cnt, e1, r2, e2 = peer_route(ht, pw["wqt"], pw["skbd"], layer)
    return peer_dense(ht, pw["u"], pw["vt"], layer, cnt, e1, r2, e2)


def _sgu_layer(x, h, gate, w, *, seq_len, emit_v):
    e = w["w_out"][0].shape[1]
    u, v = matmul(h, w["w_in"], w["b_in"], "sgu_in", col_offsets=(0, e), n_out=e,
                  out_dtypes=(BF16, F32), tm_pref=1024, tn_pref=1024)
    res = sgu_mix(v, u, w["ln_g"], w["ln_b"], w["ws"], w["mask"], w["bst"], emit_v=emit_v)
    gated = res[0]
    xn = matmul(gated, w["w_out"], None, "residual", n_out=x.shape[1], out_dtypes=(F32,),
                tm_pref=512, tn_pref=512, x=x, gate=gate, seq_len=seq_len)
    return xn, (res[1] if emit_v else None)


def _conv_layer(x, h, gate, w, hist, *, seq_len, mod_seq):
    d = x.shape[1]
    glu = matmul(h, w["w_in"], w["b_in"], "glu", col_offsets=(0, d), n_out=d, out_dtypes=(F32,),
                 tm_pref=1024, tn_pref=512)
    act = conv_core(glu, hist, w["dw"], w["dw_b"], w["ln_g"], w["ln_b"], seq_len=seq_len)
    xn = matmul(act, w["w_out"], w["b_out"], "residual", n_out=d, out_dtypes=(F32,),
                tm_pref=1024, tn_pref=512, x=x, gate=gate, seq_len=mod_seq)
    return xn, glu


def _attn_layer(x, h, gate, w, caches, *, batch, seq_len, mod_seq, q_dim, head_dim, window):
    d = x.shape[1]
    n_qkv = w["w_qkv"][0].shape[2]
    qkv = matmul(h, w["w_qkv"], w["b_qkv"], "plain", n_out=n_qkv, out_dtypes=(F32,),
                 tm_pref=1024, tn_pref=512)
    kv_dim = (n_qkv - q_dim) // 2
    assert q_dim % kv_dim == 0
    col = {"k": q_dim // kv_dim, "v": q_dim // kv_dim + 1}
    if caches is None:
        nq = CHUNK
        nc = seq_len // nq
        back = window // nq

        def pieces(which):
            return [(qkv, pl.BlockSpec((nq, kv_dim), lambda bi, c, j=j, cb=col[which]:
                                       (bi * nc + jnp.maximum(c - back + j, 0), cb)))
                    for j in range(back + 1)]
        k_pieces, v_pieces = pieces("k"), pieces("v")
    else:
        nq, nc = seq_len, 1

        def pieces(which, cache):
            return [(cache, pl.BlockSpec((1, window, kv_dim), lambda bi, c: (bi, 0, 0))),
                    (qkv, pl.BlockSpec((nq, kv_dim), lambda bi, c, cb=col[which]: (bi, cb)))]
        k_pieces, v_pieces = pieces("k", caches[0]), pieces("v", caches[1])
    o = attention(qkv, k_pieces, v_pieces, w["sinks"], grid=(batch, nc), nq=nq, q_dim=q_dim,
                  head_dim=head_dim, window=window, mask_front=caches is None)
    xn = matmul(o, w["w_o"], None, "residual", n_out=d, out_dtypes=(F32,),
                tm_pref=1024, tn_pref=512, x=x, gate=gate, seq_len=mod_seq)
    k_new = qkv[:, q_dim:q_dim + kv_dim].reshape(batch, seq_len, kv_dim)
    v_new = qkv[:, q_dim + kv_dim:].reshape(batch, seq_len, kv_dim)
    if caches is not None:
        k_new = jnp.concatenate([caches[0], k_new], axis=1)
        v_new = jnp.concatenate([caches[1], v_new], axis=1)
    return xn, k_new[:, -window:], v_new[:, -window:]


def kernel(x_prompt, x_sample, cache_k_win, cache_v_win, state_conv, c_prompt, c_sample, norm_mix_g, norm_ch_g, norm_final_g, ada_w, ada_b, sgu_w_in, sgu_b_in, sgu_ln_g, sgu_ln_b, sgu_w_s, sgu_b_s, sgu_w_out, conv_w_in, conv_b_in, conv_dw, conv_dw_b, conv_ln_g, conv_ln_b, conv_w_out, conv_b_out, attn_w_qkv, attn_b_qkv, attn_sinks, attn_w_o, peer_w_q, peer_subkeys, peer_u, peer_v):
    bp, sp, d = x_prompt.shape
    bs, ss, _ = x_sample.shape
    depth = ada_w.shape[0]
    window, kv_heads, head_dim = cache_k_win.shape[2:]
    kv_dim = kv_heads * head_dim
    q_dim = attn_w_qkv.shape[2] - 2 * kv_dim
    conv_w = conv_dw.shape[1]
    peer_heads = peer_subkeys.shape[1]
    tp, ts = bp * sp, bs * ss
    assert ts == SGU_CHUNK and sp % SGU_CHUNK == 0 and sp % CHUNK == 0 and window % CHUNK == 0

    n_c = bp + bs
    rows = -(-n_c // 8) * 8
    c_all = jnp.concatenate([c_prompt, c_sample, jnp.zeros((rows - n_c, d), F32)], axis=0)
    mods = ada_all(c_all, ada_w, ada_b)

    def mod_vectors(layer):
        m = mods[layer].reshape(rows, 6, d)
        prompt = [m[:bp, k][:, None, :] for k in range(6)]
        sample = [jnp.repeat(m[bp:n_c, k], ss, axis=0)[None] for k in range(6)]
        return prompt, sample

    t_idx = jnp.arange(SGU_CHUNK)
    mask_p = ((t_idx[None, :] // CHUNK) <= (t_idx[:, None] // CHUNK)).astype(F32)
    blk = t_idx // ss
    mask_s = ((blk[None, :] == blk[:, None])
              & ((t_idx[None, :] % ss) // CHUNK <= (t_idx[:, None] % ss) // CHUNK)).astype(F32)

    sgu_w_in_bf, sgu_w_out_bf = sgu_w_in.astype(BF16), sgu_w_out.astype(BF16)
    conv_w_in_bf, conv_w_out_bf = conv_w_in.astype(BF16), conv_w_out.astype(BF16)
    attn_w_qkv_bf, attn_w_o_bf = attn_w_qkv.astype(BF16), attn_w_o.astype(BF16)
    nk, half = peer_subkeys.shape[3:]
    zk = jnp.zeros((depth, peer_heads, nk, half), F32)
    skbd = jnp.concatenate([jnp.concatenate([peer_subkeys[:, :, 0], zk], axis=3),
                            jnp.concatenate([zk, peer_subkeys[:, :, 1]], axis=3)], axis=2).astype(BF16)
    pw = dict(wqt=jnp.swapaxes(peer_w_q, 1, 2).astype(BF16), skbd=skbd,
              u=peer_u.astype(BF16), vt=jnp.swapaxes(peer_v, 1, 2).astype(BF16))

    xp = x_prompt.reshape(tp, d)
    xs = x_sample.reshape(ts, d)
    pend_p = pend_s = None
    ia = ib = ic = 0
    conv_p, conv_s, kwin_p, vwin_p, kwin_s, vwin_s, sgu_s = [], [], [], [], [], [], []
    for layer in range(depth):
        mp, ms = mod_vectors(layer)
        hp = norm(xp, norm_mix_g[layer], mp[1], mp[0], seq_len=sp, pending=pend_p, emit_x=pend_p is not None)
        hs = norm(xs, norm_mix_g[layer], ms[1], ms[0], seq_len=ts, pending=pend_s, emit_x=pend_s is not None)
        if pend_p is not None:
            (hp, xp), (hs, xs) = hp, hs
        kind = layer % N_MIXERS
        if kind == 0:
            w = dict(w_in=(sgu_w_in_bf, ia), b_in=sgu_b_in[ia], ln_g=sgu_ln_g[ia],
                     ln_b=sgu_ln_b[ia], w_out=(sgu_w_out_bf, ia))
            ws = sgu_w_s[ia]
            wp = dict(w, ws=ws, mask=mask_p, bst=sgu_b_s[ia].T)
            reps = SGU_CHUNK // ss
            wsm = dict(w, ws=jnp.tile(ws[:, :ss, :ss], (1, reps, reps)), mask=mask_s,
                       bst=jnp.tile(sgu_b_s[ia][:, :ss], (1, reps)).T)
            xp, _ = _sgu_layer(xp, hp, mp[2], wp, seq_len=sp, emit_v=False)
            xs, v_rows = _sgu_layer(xs, hs, ms[2], wsm, seq_len=ts, emit_v=True)
            sgu_s.append(v_rows.reshape(bs, ss, -1))
            ia += 1
        elif kind == 1:
            dw = conv_dw[ib]
            w = dict(w_in=(conv_w_in_bf, ib), b_in=conv_b_in[ib], dw=dw, dw_b=conv_dw_b[ib],
                     ln_g=conv_ln_g[ib], ln_b=conv_ln_b[ib], w_out=(conv_w_out_bf, ib),
                     b_out=conv_b_out[ib])
            pad = HALO - (conv_w - 1)
            hist_p = jnp.zeros((bp, HALO, d), F32)
            hist_s = jnp.pad(state_conv[ib], ((0, 0), (pad, 0), (0, 0)))
            xp, glu_p = _conv_layer(xp, hp, mp[2], w, hist_p, seq_len=sp, mod_seq=sp)
            xs, glu_s = _conv_layer(xs, hs, ms[2], w, hist_s, seq_len=ss, mod_seq=ts)
            conv_p.append(glu_p.reshape(bp, sp, d)[:, sp - (conv_w - 1):])
            conv_s.append(jnp.concatenate([state_conv[ib], glu_s.reshape(bs, ss, d)], axis=1)[:, -(conv_w - 1):])
            ib += 1
        else:
            w = dict(w_qkv=(attn_w_qkv_bf, ic), b_qkv=attn_b_qkv[ic], sinks=attn_sinks[ic],
                     w_o=(attn_w_o_bf, ic))
            caches = (cache_k_win[ic].reshape(bs, window, kv_dim), cache_v_win[ic].reshape(bs, window, kv_dim))
            common = dict(q_dim=q_dim, head_dim=head_dim, window=window)
            xp, kp, vp = _attn_layer(xp, hp, mp[2], w, None, batch=bp, seq_len=sp, mod_seq=sp, **common)
            xs, ks, vs = _attn_layer(xs, hs, ms[2], w, caches, batch=bs, seq_len=ss, mod_seq=ts, **common)
            kwin_p.append(kp.reshape(bp, window, kv_heads, head_dim))
            vwin_p.append(vp.reshape(bp, window, kv_heads, head_dim))
            kwin_s.append(ks.reshape(bs, window, kv_heads, head_dim))
            vwin_s.append(vs.reshape(bs, window, kv_heads, head_dim))
            ic += 1

        pend_p = (_peer_layer(xp, norm_ch_g[layer], mp[4], mp[3], pw, layer, seq_len=sp), mp[5])
        pend_s = (_peer_layer(xs, norm_ch_g[layer], ms[4], ms[3], pw, layer, seq_len=ts), ms[5])

    y_prompt = norm(xp, norm_final_g, seq_len=sp, out_dtype=F32, pending=pend_p).reshape(bp, sp, d)
    y_sample = norm(xs, norm_final_g, seq_len=ts, out_dtype=F32, pending=pend_s).reshape(bs, ss, d)
    return (y_prompt, y_sample, jnp.stack(conv_p), jnp.stack(kwin_p), jnp.stack(vwin_p),
            jnp.stack(conv_s), jnp.stack(kwin_s), jnp.stack(vwin_s), jnp.stack(sgu_s))
```

```python
import functools

import jax
import jax.numpy as jnp
from jax import lax
from jax.experimental import pallas as pl
from jax.experimental.pallas import tpu as pltpu

F32 = jnp.float32
BF16 = jnp.bfloat16

EPS = 1e-6
CHUNK = 64
SGU_CHUNK = 128
SGU_GROUPS = 8
PEER_TOPK = 16
NEG_INF = -1e30
N_MIXERS = 3

V7X_VMEM_BYTES = 64 * 1024 * 1024
VMEM_LIMIT = V7X_VMEM_BYTES - 8 * 1024 * 1024
LANE = 128
SUBLANES = 8


def _cp(*sem):
    return pltpu.CompilerParams(dimension_semantics=sem, vmem_limit_bytes=VMEM_LIMIT)


def _tile(n, pref, mult=LANE):
    if n <= pref:
        return n
    t = (pref // mult) * mult
    while t >= mult:
        if n % t == 0:
            return t
        t -= mult
    return n


def _gelu(x):
    return jax.nn.gelu(x, approximate=True)


def _ada_kernel(c_ref, w_ref, b_ref, o_ref):
    c = c_ref[...]
    a = (c * jax.nn.sigmoid(c)).astype(BF16)
    o_ref[0] = jnp.dot(a, w_ref[0].astype(BF16), preferred_element_type=F32) + b_ref[0]


def ada_all(c_all, ada_w, ada_b):
    n_layers, d, n = ada_w.shape
    rows = c_all.shape[0]
    tn = _tile(n, 1024)
    return pl.pallas_call(
        _ada_kernel,
        grid=(n_layers, n // tn),
        in_specs=[pl.BlockSpec((rows, d), lambda l, j: (0, 0)),
                  pl.BlockSpec((1, d, tn), lambda l, j: (l, 0, j)),
                  pl.BlockSpec((1, 1, tn), lambda l, j: (l, 0, j))],
        out_specs=pl.BlockSpec((1, rows, tn), lambda l, j: (l, 0, j)),
        out_shape=jax.ShapeDtypeStruct((n_layers, rows, n), F32),
        compiler_params=_cp("arbitrary", "arbitrary"),
        name="ada",
    )(c_all, ada_w, ada_b.reshape(n_layers, 1, n))


def _mod_spec(arr, tm, seq_len, tn=None, col_of=None):
    _, r, d = arr.shape
    if tn is None:
        return pl.BlockSpec((1, r, d), lambda i, *_: ((i * tm) // seq_len, 0, 0))
    return pl.BlockSpec((1, r, tn), lambda i, j: ((i * tm) // seq_len, 0, col_of(j)))


def _norm_kernel(*refs, modulated, transposed, pending, emit_x):
    refs = list(refs)
    x = refs.pop(0)[...]
    if pending:
        yt_ref, pg_ref = refs.pop(0), refs.pop(0)
        x = x + pg_ref[0] * yt_ref[...].T
    g_ref = refs.pop(0)
    y = x * lax.rsqrt(jnp.mean(x * x, axis=-1, keepdims=True) + EPS) * g_ref[...]
    if modulated:
        sc_ref, sh_ref = refs.pop(0), refs.pop(0)
        y = y * (1.0 + sc_ref[0]) + sh_ref[0]
    o_ref = refs.pop(0)
    if transposed:
        o_ref[...] = y.T.astype(o_ref.dtype)
    else:
        o_ref[...] = y.astype(o_ref.dtype)
    if emit_x:
        refs.pop(0)[...] = x


def norm(x, g, scale=None, shift=None, *, seq_len, transposed=False, out_dtype=BF16, pending=None,
         emit_x=False):
    t, d = x.shape
    tm = _tile(seq_len, 512)
    modulated = scale is not None
    row = pl.BlockSpec((tm, d), lambda i: (i, 0))
    in_specs, args = [row], [x]
    if pending is not None:
        in_specs += [pl.BlockSpec((d, tm), lambda i: (0, i)), _mod_spec(pending[1], tm, seq_len)]
        args += list(pending)
    in_specs.append(pl.BlockSpec((1, d), lambda i: (0, 0)))
    args.append(g.reshape(1, d))
    if modulated:
        in_specs += [_mod_spec(scale, tm, seq_len), _mod_spec(shift, tm, seq_len)]
        args += [scale, shift]
    if transposed:
        out_specs = [pl.BlockSpec((d, tm), lambda i: (0, i))]
        out_shape = [jax.ShapeDtypeStruct((d, t), out_dtype)]
    else:
        out_specs = [row]
        out_shape = [jax.ShapeDtypeStruct((t, d), out_dtype)]
    if emit_x:
        out_specs.append(row)
        out_shape.append(jax.ShapeDtypeStruct((t, d), F32))
    res = pl.pallas_call(
        functools.partial(_norm_kernel, modulated=modulated, transposed=transposed,
                          pending=pending is not None, emit_x=emit_x),
        grid=(t // tm,), in_specs=in_specs, out_specs=out_specs, out_shape=out_shape,
        compiler_params=_cp("arbitrary"), name="norm",
    )(*args)
    return res if emit_x else res[0]


def _mm_kernel(*refs, n_w, has_bias, epilogue):
    h_ref = refs[0]
    w_refs = refs[1:1 + n_w]
    pos = 1 + n_w
    b_refs = refs[pos:pos + n_w] if has_bias else ()
    pos += n_w if has_bias else 0
    h = h_ref[...]
    accs = []
    for k in range(n_w):
        acc = jnp.dot(h, w_refs[k][...], preferred_element_type=F32)
        if has_bias:
            acc = acc + b_refs[k][...]
        accs.append(acc)
    rest = refs[pos:]
    if epilogue == "sgu_in":
        rest[0][...] = _gelu(accs[0]).astype(BF16)
        rest[1][...] = _gelu(accs[1])
    elif epilogue == "glu":
        rest[0][...] = accs[0] * jax.nn.sigmoid(accs[1])
    elif epilogue == "plain":
        rest[0][...] = accs[0]
    elif epilogue == "residual":
        x_ref, gate_ref, o_ref = rest
        o_ref[...] = x_ref[...] + gate_ref[0] * accs[0]
    else:
        raise ValueError(epilogue)


def matmul(h, w, bias, epilogue, *, col_offsets=(0,), n_out, out_dtypes, tm_pref, tn_pref,
           x=None, gate=None, seq_len=None):
    w, layer = w
    t, k_dim = h.shape
    tm = _tile(t if seq_len is None else seq_len, tm_pref, 8)
    tn = _tile(n_out, tn_pref)
    n_w = len(col_offsets)
    has_bias = bias is not None
    in_specs = [pl.BlockSpec((tm, k_dim), lambda i, j: (i, 0))]
    args = [h]
    for off in col_offsets:
        assert off % tn == 0
        in_specs.append(pl.BlockSpec((None, k_dim, tn), lambda i, j, o=off // tn: (layer, 0, j + o)))
        args.append(w)
    if has_bias:
        b2 = bias.reshape(1, -1)
        for off in col_offsets:
            in_specs.append(pl.BlockSpec((1, tn), lambda i, j, o=off // tn: (0, j + o)))
            args.append(b2)
    if epilogue == "residual":
        in_specs += [pl.BlockSpec((tm, tn), lambda i, j: (i, j)),
                     _mod_spec(gate, tm, seq_len, tn, lambda j: j)]
        args += [x, gate]
    out_specs = [pl.BlockSpec((tm, tn), lambda i, j: (i, j)) for _ in out_dtypes]
    out_shape = [jax.ShapeDtypeStruct((t, n_out), dt) for dt in out_dtypes]
    res = pl.pallas_call(
        functools.partial(_mm_kernel, n_w=n_w, has_bias=has_bias, epilogue=epilogue),
        grid=(t // tm, n_out // tn), in_specs=in_specs, out_specs=out_specs, out_shape=out_shape,
        compiler_params=_cp("arbitrary", "arbitrary"), name="mm_" + epilogue,
    )(*args)
    return res if len(res) > 1 else res[0]


def _sgu_mix_kernel(v_ref, u_ref, lg_ref, lb_ref, ws_ref, mask_ref, bst_ref, o_ref, *vout,
                    groups):
    v = v_ref[...]
    mu = jnp.mean(v, axis=-1, keepdims=True)
    vc = v - mu
    var = jnp.mean(vc * vc, axis=-1, keepdims=True)
    vn = vc * lax.rsqrt(var + EPS) * lg_ref[...] + lb_ref[...]
    if vout:
        vout[0][...] = vn
    eg = v.shape[1] // groups
    for g in range(groups):
        w = (ws_ref[g] * mask_ref[...]).astype(BF16)
        vg = vn[:, g * eg:(g + 1) * eg].astype(BF16)
        mixed = jnp.dot(w, vg, preferred_element_type=F32) + bst_ref[:, g:g + 1]
        u = u_ref[:, g * eg:(g + 1) * eg].astype(F32)
        o_ref[:, g * eg:(g + 1) * eg] = (u * mixed).astype(BF16)


def sgu_mix(v, u, ln_g, ln_b, ws, mask, bst, *, emit_v):
    t, e = v.shape
    n = ws.shape[1]
    groups = ws.shape[0]
    row = lambda i: (i, 0)
    full2 = lambda i: (0, 0)
    out_specs = [pl.BlockSpec((n, e), row)]
    out_shape = [jax.ShapeDtypeStruct((t, e), BF16)]
    if emit_v:
        out_specs.append(pl.BlockSpec((n, e), row))
        out_shape.append(jax.ShapeDtypeStruct((t, e), F32))
    return pl.pallas_call(
        functools.partial(_sgu_mix_kernel, groups=groups),
        grid=(t // n,),
        in_specs=[pl.BlockSpec((n, e), row), pl.BlockSpec((n, e), row),
                  pl.BlockSpec((1, e), full2), pl.BlockSpec((1, e), full2),
                  pl.BlockSpec((groups, n, n), lambda i: (0, 0, 0)),
                  pl.BlockSpec((n, n), full2), pl.BlockSpec((n, groups), full2)],
        out_specs=out_specs, out_shape=out_shape,
        compiler_params=_cp("arbitrary"), name="sgu_mix",
    )(v, u, ln_g.reshape(1, e), ln_b.reshape(1, e), ws, mask, bst)


HALO = 32
CONV_COLS = 128


def _conv_kernel(*refs, tt, width, single):
    if single:
        cur_ref, hist_ref, dw_ref, dwb_ref, lg_ref, lb_ref, o_ref, xp_ref, y_ref, sh_ref = refs
    else:
        cur_ref, prev_ref, hist_ref, dw_ref, dwb_ref, lg_ref, lb_ref, o_ref, xp_ref, y_ref, sh_ref = refs
    d = cur_ref.shape[1]
    xp_ref[HALO:HALO + tt, :] = cur_ref[...]
    if single:
        xp_ref[0:HALO, :] = hist_ref[0]
    else:
        first = pl.program_id(1) == 0

        @pl.when(first)
        def _():
            xp_ref[0:HALO, :] = hist_ref[0]

        @pl.when(jnp.logical_not(first))
        def _():
            xp_ref[0:HALO, :] = prev_ref[...]

    base = HALO - (width - 1)
    n_b = min(SUBLANES, width)
    n_a = [(width - 1 - b) // SUBLANES + 1 for b in range(n_b)]
    for b in range(n_b):
        rows = tt + SUBLANES * (n_a[b] - 1)
        sh_ref[b, 0:rows, :] = xp_ref[base + b:base + b + rows, :]
    cw = min(CONV_COLS, d)
    for c in range(d // cw):
        cols = slice(c * cw, (c + 1) * cw)
        acc = jnp.zeros((tt, cw), F32)
        for b in range(n_b):
            for a in range(n_a[b]):
                j = SUBLANES * a + b
                acc = acc + dw_ref[j:j + 1, cols] * sh_ref[b, SUBLANES * a:SUBLANES * a + tt, cols]
        y_ref[:, cols] = acc + dwb_ref[:, cols]
    y = y_ref[...]
    mu = jnp.mean(y, axis=-1, keepdims=True)
    yc = y - mu
    var = jnp.mean(yc * yc, axis=-1, keepdims=True)
    yn = yc * lax.rsqrt(var + EPS) * lg_ref[...] + lb_ref[...]
    o_ref[...] = (yn * jax.nn.sigmoid(yn)).astype(BF16)


def conv_core(glu, hist, dw, dw_b, ln_g, ln_b, *, seq_len):
    t, d = glu.shape
    b = t // seq_len
    width = dw.shape[0]
    tt = _tile(seq_len, 128, HALO)
    single = tt == seq_len
    nb = seq_len // tt
    vec = lambda a: a.reshape(1, d)
    full2 = lambda bi, i: (0, 0)
    in_specs = [pl.BlockSpec((tt, d), lambda bi, i: (bi * nb + i, 0))]
    args = [glu]
    if not single:
        per = tt // HALO
        in_specs.append(pl.BlockSpec((HALO, d), lambda bi, i: (jnp.maximum((bi * nb + i) * per - 1, 0), 0)))
        args.append(glu)
    in_specs += [pl.BlockSpec((1, HALO, d), lambda bi, i: (bi, 0, 0)),
                 pl.BlockSpec((width, d), full2),
                 pl.BlockSpec((1, d), full2), pl.BlockSpec((1, d), full2), pl.BlockSpec((1, d), full2)]
    args += [hist, dw, vec(dw_b), vec(ln_g), vec(ln_b)]
    return pl.pallas_call(
        functools.partial(_conv_kernel, tt=tt, width=width, single=single),
        grid=(b, nb), in_specs=in_specs,
        out_specs=pl.BlockSpec((tt, d), lambda bi, i: (bi * nb + i, 0)),
        out_shape=jax.ShapeDtypeStruct((t, d), BF16),
        scratch_shapes=[pltpu.VMEM((HALO + tt, d), F32), pltpu.VMEM((tt, d), F32),
                        pltpu.VMEM((min(SUBLANES, width), tt + HALO, d), F32)],
        compiler_params=_cp("arbitrary", "arbitrary"), name="conv_core",
    )(*args)


def _attn_kernel(sink_ref, q_ref, *rest, n_pieces, qpc, kv_heads, q_per_kv, head_dim, window, n_heads,
                 mask_front):
    k_refs = rest[:n_pieces]
    v_refs = rest[n_pieces:2 * n_pieces]
    o_ref, k_scr, v_scr = rest[2 * n_pieces:]
    off = 0
    for kr, vr in zip(k_refs, v_refs):
        rows = kr.shape[-2]
        k_scr[off:off + rows, :] = kr[...].reshape(kr.shape[-2:]).astype(BF16)
        v_scr[off:off + rows, :] = vr[...].reshape(vr.shape[-2:]).astype(BF16)
        off += rows
    nq = q_ref.shape[0] // qpc
    nk = k_scr.shape[0] - (qpc - 1) * nq
    qi = lax.broadcasted_iota(jnp.int32, (nq, nk), 0)
    kj = lax.broadcasted_iota(jnp.int32, (nq, nk), 1)
    dist = jnp.abs(window + qi - kj).astype(F32)
    scale = head_dim ** -0.5
    groups = [range(g * q_per_kv, (g + 1) * q_per_kv) for g in range(kv_heads)]
    chains = [(s, g) for s in range(qpc) for g in range(kv_heads)]
    scores, sinks = [], []
    for s, g in chains:
        qrows = slice(s * nq, (s + 1) * nq)
        q = jnp.concatenate([q_ref[qrows, hd * head_dim:(hd + 1) * head_dim] for hd in groups[g]], axis=0)
        k = k_scr[s * nq:s * nq + nk, g * head_dim:(g + 1) * head_dim]
        sc = lax.dot_general(q.astype(BF16), k, (((1,), (1,)), ((), ())), preferred_element_type=F32)
        if mask_front:
            valid = kj >= window - (pl.program_id(1) * qpc + s) * nq
        bias = []
        for hd in groups[g]:
            b = (2.0 ** (-8.0 * (hd + 1) / n_heads)) * dist
            bias.append(jnp.where(valid, b, -NEG_INF) if mask_front else b)
        scores.append(sc * scale - jnp.concatenate(bias, axis=0))
        sinks.append(jnp.concatenate([jnp.full((nq, 1), sink_ref[hd], F32) for hd in groups[g]], axis=0))
    maxes = [jnp.maximum(jnp.max(sc, axis=-1, keepdims=True), sk) for sc, sk in zip(scores, sinks)]
    probs = [jnp.exp(sc - m) for sc, m in zip(scores, maxes)]
    dens = [jnp.sum(p, axis=-1, keepdims=True) + jnp.exp(sk - m) for p, sk, m in zip(probs, sinks, maxes)]
    for i, (s, g) in enumerate(chains):
        v = v_scr[s * nq:s * nq + nk, g * head_dim:(g + 1) * head_dim]
        o = jnp.dot((probs[i] / dens[i]).astype(BF16), v, preferred_element_type=F32).astype(BF16)
        for r, hd in enumerate(groups[g]):
            o_ref[s * nq:(s + 1) * nq, hd * head_dim:(hd + 1) * head_dim] = o[r * nq:(r + 1) * nq]


def attention(qkv, k_pieces, v_pieces, sinks, *, grid, nq, qpc, q_dim, head_dim, window, mask_front):
    t = qkv.shape[0]
    nc = grid[1]
    kv_dim = k_pieces[0][1].block_shape[-1]
    band = sum(spec.block_shape[-2] for _, spec in k_pieces)
    n_heads = q_dim // head_dim
    kv_heads = kv_dim // head_dim
    kern = functools.partial(_attn_kernel, n_pieces=len(k_pieces), qpc=qpc, kv_heads=kv_heads,
                             q_per_kv=n_heads // kv_heads, head_dim=head_dim, window=window,
                             n_heads=n_heads, mask_front=mask_front)
    pieces = k_pieces + v_pieces
    return pl.pallas_call(
        kern, grid=grid,
        in_specs=[pl.BlockSpec(memory_space=pltpu.SMEM),
                  pl.BlockSpec((nq, q_dim), lambda bi, c: (bi * nc + c, 0))] + [sp for _, sp in pieces],
        out_specs=pl.BlockSpec((nq, q_dim), lambda bi, c: (bi * nc + c, 0)),
        out_shape=jax.ShapeDtypeStruct((t, q_dim), BF16),
        scratch_shapes=[pltpu.VMEM((band, kv_dim), BF16), pltpu.VMEM((band, kv_dim), BF16)],
        compiler_params=_cp("arbitrary", "arbitrary"), name="attention",
    )(sinks, qkv, *[a for a, _ in pieces])


def _sort_network(n):
    pairs = []
    p = 1
    while p < n:
        k = p
        while k >= 1:
            for j in range(k % p, n - k, 2 * k):
                for i in range(min(k, n - j - k)):
                    if (i + j) // (2 * p) == (i + j + k) // (2 * p):
                        pairs.append((i + j, i + j + k))
            k //= 2
        p *= 2
    return pairs


def _top_values(s, scr, u, topk):
    n = s.shape[0] // SUBLANES
    assert n >= topk and n & (n - 1) == 0
    vs = [s[SUBLANES * g:SUBLANES * (g + 1), :] for g in range(n)]
    for i, j in _sort_network(n):
        vs[i], vs[j] = jnp.maximum(vs[i], vs[j]), jnp.minimum(vs[i], vs[j])
    for r in range(topk):
        m = jnp.max(vs[0], axis=0, keepdims=True)
        scr[u, r:r + 1, :] = m
        hit = vs[0] == m
        for k in range(topk - r - 1):
            vs[k] = jnp.where(hit, vs[k + 1], vs[k])


def _topk_tables(s1, s2, a_scr, b_scr, c_scr, u, *, topk, n_cand):
    _top_values(s1, a_scr, u, topk)
    _top_values(s2, b_scr, u, topk)
    rank2 = jnp.full(s2.shape, topk + 1.0, F32)
    for j in reversed(range(topk)):
        rank2 = jnp.where(s2 >= b_scr[u, j:j + 1, :], j + 1.0, rank2)
    k = 0
    for r in range(topk):
        for j in range(topk // (r + 1)):
            c_scr[u, k:k + 1, :] = a_scr[u, r:r + 1, :] + b_scr[u, j:j + 1, :]
            k += 1
    rows = c_scr.shape[1]
    c_scr[u, n_cand:rows, :] = jnp.full((rows - n_cand, c_scr.shape[2]), NEG_INF, F32)
    cand = c_scr[u]
    ridx = lax.broadcasted_iota(jnp.int32, cand.shape, 0)
    m0 = jnp.max(cand, axis=0, keepdims=True)
    z = jnp.zeros_like(m0)
    m = m0
    for r in range(topk):
        m = jnp.max(cand, axis=0, keepdims=True)
        z = z + jnp.exp(m - m0)
        first = jnp.min(jnp.where(cand == m, ridx, rows), axis=0, keepdims=True)
        cand = jnp.where(ridx == first, NEG_INF, cand)
    tau = m
    a = a_scr[u]
    cnt = jnp.zeros_like(s1)
    for j in range(topk):
        ok = a + b_scr[u, j:j + 1, :] >= tau
        theta = jnp.min(jnp.where(ok, a, -NEG_INF), axis=0, keepdims=True)
        cnt = jnp.where(s1 >= theta, j + 1.0, cnt)
    e1 = jnp.exp(s1 - a_scr[u, 0:1, :])
    e2 = jnp.exp(s2 - b_scr[u, 0:1, :]) / z
    return cnt, e1, rank2, e2


def _peer_route_kernel(ht_ref, wqt_ref, sk_ref, cnt_ref, e1_ref, r2_ref, e2_ref, a_scr, b_scr, c_scr,
                       *, topk, n_cand, hps):
    hk = sk_ref.shape[1]
    nk = hk // 2
    nl = ht_ref.shape[1] // LANE
    def head_scores(hh):
        qt = jnp.dot(wqt_ref[hh * hk:(hh + 1) * hk, :], ht_ref[...], preferred_element_type=F32)
        return jnp.dot(sk_ref[hh], qt.astype(BF16), preferred_element_type=F32)

    scores = [head_scores(0)]
    for hh in range(hps):
        if hh + 1 < hps:
            scores.append(head_scores(hh + 1))
        for l in range(nl):
            lanes = slice(l * LANE, (l + 1) * LANE)
            cnt, e1, rank2, e2 = _topk_tables(scores[hh][0:nk, lanes], scores[hh][nk:hk, lanes],
                                              a_scr, b_scr, c_scr, hh * nl + l, topk=topk, n_cand=n_cand)
            cnt_ref[hh, l] = cnt
            e1_ref[hh, l] = e1
            r2_ref[hh, l] = rank2.astype(r2_ref.dtype)
            e2_ref[hh, l] = e2.astype(e2_ref.dtype)


def peer_route(ht, wqt, skbd, layer):
    d, t = ht.shape
    heads, hk = skbd.shape[1:3]
    nk = hk // 2
    tt = _tile(t, 512)
    nl = tt // LANE
    hps = 2
    n_cand = sum(PEER_TOPK // (r + 1) for r in range(PEER_TOPK))
    cand_rows = -(-n_cand // 8) * 8
    units = hps * nl
    out32 = jax.ShapeDtypeStruct((heads, t // LANE, nk, LANE), F32)
    out16 = jax.ShapeDtypeStruct((heads, t // LANE, nk, LANE), BF16)
    spec = pl.BlockSpec((hps, nl, nk, LANE), lambda i, h: (h, i, 0, 0))
    return pl.pallas_call(
        functools.partial(_peer_route_kernel, topk=PEER_TOPK, n_cand=n_cand, hps=hps),
        grid=(t // tt, heads // hps),
        in_specs=[pl.BlockSpec((d, tt), lambda i, h: (0, i)),
                  pl.BlockSpec((None, hps * hk, d), lambda i, h: (layer, h, 0)),
                  pl.BlockSpec((None, hps, hk, hk), lambda i, h: (layer, h, 0, 0))],
        out_specs=[spec, spec, spec, spec], out_shape=[out32, out32, out16, out16],
        scratch_shapes=[pltpu.VMEM((units, PEER_TOPK, LANE), F32), pltpu.VMEM((units, PEER_TOPK, LANE), F32),
                        pltpu.VMEM((units, cand_rows, LANE), F32)],
        compiler_params=_cp("arbitrary", "arbitrary"), name="peer_route",
    )(ht, wqt, skbd)


def _peer_dense_kernel(ht_ref, u_ref, vt_ref, cnt_ref, e1_ref, r2_ref, e2_ref, o_ref, a_scr, w_scr,
                       *, heads, n_i1, nk, tb, tc):
    @pl.when(pl.program_id(1) == 0)
    def _():
        o_ref[...] = jnp.zeros_like(o_ref)

    def row_bf16(ref, h, l, il):
        return jnp.broadcast_to(ref[h, l, il:il + 1, :], (nk, LANE)).astype(BF16)

    lpc = tc // LANE
    ec, d = u_ref.shape
    def expert_inputs(c):
        cols = slice(c * tc, (c + 1) * tc)
        for half in (slice(0, ec // 2), slice(ec // 2, ec)):
            a_scr[c, half, :] = jnp.dot(u_ref[half, :], ht_ref[:, cols], preferred_element_type=F32)

    def gated_weights(c):
        for il in range(n_i1):
            rows = slice(il * nk, (il + 1) * nk)
            parts = []
            for l in range(c * lpc, (c + 1) * lpc):
                gate = None
                for h in range(heads):
                    term = jnp.where(row_bf16(cnt_ref, h, l, il) >= r2_ref[h, l],
                                     row_bf16(e1_ref, h, l, il), 0.0) * e2_ref[h, l]
                    gate = term if gate is None else gate + term
                sub = slice((l - c * lpc) * LANE, (l - c * lpc + 1) * LANE)
                parts.append(_gelu(a_scr[c, rows, sub]).astype(BF16) * gate)
            w_scr[c, rows, :] = jnp.concatenate(parts, axis=1)

    def expert_outputs(c):
        cols = slice(c * tc, (c + 1) * tc)
        for half in (slice(0, d // 2), slice(d // 2, d)):
            o_ref[half, cols] += jnp.dot(vt_ref[half, :], w_scr[c], preferred_element_type=F32)

    n_chains = tb // tc
    for c in range(n_chains):
        expert_inputs(c)
    for c in range(n_chains):
        gated_weights(c)
        expert_outputs(c)


def peer_dense(ht, u_bf, vt_bf, layer, cnt, e1, r2, e2):
    d, t = ht.shape
    n_exp = u_bf.shape[1]
    heads, _, nk, _ = cnt.shape
    tb = _tile(t, 1024)
    tc = _tile(tb, 256)
    nl = tb // LANE
    n_i1 = 8
    ec = n_i1 * nk
    once = pl.Buffered(1)
    return pl.pallas_call(
        functools.partial(_peer_dense_kernel, heads=heads, n_i1=n_i1, nk=nk, tb=tb, tc=tc),
        grid=(t // tb, n_exp // ec),
        in_specs=[pl.BlockSpec((d, tb), lambda i, j: (0, i), pipeline_mode=once),
                  pl.BlockSpec((None, ec, d), lambda i, j: (layer, j, 0)),
                  pl.BlockSpec((None, d, ec), lambda i, j: (layer, 0, j)),
                  pl.BlockSpec((heads, nl, n_i1, LANE), lambda i, j: (0, i, j, 0)),
                  pl.BlockSpec((heads, nl, n_i1, LANE), lambda i, j: (0, i, j, 0)),
                  pl.BlockSpec((heads, nl, nk, LANE), lambda i, j: (0, i, 0, 0), pipeline_mode=once),
                  pl.BlockSpec((heads, nl, nk, LANE), lambda i, j: (0, i, 0, 0), pipeline_mode=once)],
        out_specs=pl.BlockSpec((d, tb), lambda i, j: (0, i)),
        out_shape=jax.ShapeDtypeStruct((d, t), F32),
        scratch_shapes=[pltpu.VMEM((tb // tc, ec, tc), F32), pltpu.VMEM((tb // tc, ec, tc), BF16)],
        compiler_params=_cp("arbitrary", "arbitrary"), name="peer_dense",
    )(ht, u_bf, vt_bf, cnt, e1, r2, e2)


def _peer_layer(x, g, sc, sh, pw, layer, *, seq_len):
    ht = norm(x, g, sc, sh, seq_len=seq_len, transposed=True)
    cnt, e1, r2, e2 = peer_route(ht, pw["wqt"], pw["skbd"], layer)
    return peer_dense(ht, pw["u"], pw["vt"], layer, cnt, e1, r2, e2)


def _sgu_layer(x, h, gate, w, *, seq_len, emit_v):
    e = w["w_out"][0].shape[1]
    u, v = matmul(h, w["w_in"], w["b_in"], "sgu_in", col_offsets=(0, e), n_out=e,
                  out_dtypes=(BF16, F32), tm_pref=1024, tn_pref=512)
    res = sgu_mix(v, u, w["ln_g"], w["ln_b"], w["ws"], w["mask"], w["bst"], emit_v=emit_v)
    gated = res[0]
    xn = matmul(gated, w["w_out"], None, "residual", n_out=x.shape[1], out_dtypes=(F32,),
                tm_pref=512, tn_pref=512, x=x, gate=gate, seq_len=seq_len)
    return xn, (res[1] if emit_v else None)


def _conv_layer(x, h, gate, w, hist, *, seq_len, mod_seq):
    d = x.shape[1]
    glu = matmul(h, w["w_in"], w["b_in"], "glu", col_offsets=(0, d), n_out=d, out_dtypes=(F32,),
                 tm_pref=1024, tn_pref=512)
    act = conv_core(glu, hist, w["dw"], w["dw_b"], w["ln_g"], w["ln_b"], seq_len=seq_len)
    xn = matmul(act, w["w_out"], w["b_out"], "residual", n_out=d, out_dtypes=(F32,),
                tm_pref=1024, tn_pref=512, x=x, gate=gate, seq_len=mod_seq)
    return xn, glu


def _attn_layer(x, h, gate, w, caches, *, batch, seq_len, mod_seq, q_dim, head_dim, window):
    d = x.shape[1]
    n_qkv = w["w_qkv"][0].shape[2]
    qkv = matmul(h, w["w_qkv"], w["b_qkv"], "plain", n_out=n_qkv, out_dtypes=(F32,),
                 tm_pref=1024, tn_pref=512)
    kv_dim = (n_qkv - q_dim) // 2
    assert q_dim % kv_dim == 0
    col = {"k": q_dim // kv_dim, "v": q_dim // kv_dim + 1}
    if caches is None:
        qpc = 2 if (seq_len // CHUNK) % 2 == 0 else 1
        nq = qpc * CHUNK
        nc = seq_len // nq
        back = window // CHUNK
        n_all = seq_len // CHUNK

        def pieces(which):
            return [(qkv, pl.BlockSpec((CHUNK, kv_dim), lambda bi, c, j=j, cb=col[which]:
                                       (bi * n_all + jnp.maximum(c * qpc - back + j, 0), cb)))
                    for j in range(back + qpc)]
        k_pieces, v_pieces = pieces("k"), pieces("v")
    else:
        nq, nc, qpc = seq_len, 1, 1

        def pieces(which, cache):
            return [(cache, pl.BlockSpec((1, window, kv_dim), lambda bi, c: (bi, 0, 0))),
                    (qkv, pl.BlockSpec((nq, kv_dim), lambda bi, c, cb=col[which]: (bi, cb)))]
        k_pieces, v_pieces = pieces("k", caches[0]), pieces("v", caches[1])
    o = attention(qkv, k_pieces, v_pieces, w["sinks"], grid=(batch, nc), nq=nq, qpc=qpc, q_dim=q_dim,
                  head_dim=head_dim, window=window, mask_front=caches is None)
    xn = matmul(o, w["w_o"], None, "residual", n_out=d, out_dtypes=(F32,),
                tm_pref=1024, tn_pref=512, x=x, gate=gate, seq_len=mod_seq)
    k_new = qkv[:, q_dim:q_dim + kv_dim].reshape(batch, seq_len, kv_dim)
    v_new = qkv[:, q_dim + kv_dim:].reshape(batch, seq_len, kv_dim)
    if caches is not None:
        k_new = jnp.concatenate([caches[0], k_new], axis=1)
        v_new = jnp.concatenate([caches[1], v_new], axis=1)
    return xn, k_new[:, -window:], v_new[:, -window:]


def kernel(x_prompt, x_sample, cache_k_win, cache_v_win, state_conv, c_prompt, c_sample, norm_mix_g, norm_ch_g, norm_final_g, ada_w, ada_b, sgu_w_in, sgu_b_in, sgu_ln_g, sgu_ln_b, sgu_w_s, sgu_b_s, sgu_w_out, conv_w_in, conv_b_in, conv_dw, conv_dw_b, conv_ln_g, conv_ln_b, conv_w_out, conv_b_out, attn_w_qkv, attn_b_qkv, attn_sinks, attn_w_o, peer_w_q, peer_subkeys, peer_u, peer_v):
    bp, sp, d = x_prompt.shape
    bs, ss, _ = x_sample.shape
    depth = ada_w.shape[0]
    window, kv_heads, head_dim = cache_k_win.shape[2:]
    kv_dim = kv_heads * head_dim
    q_dim = attn_w_qkv.shape[2] - 2 * kv_dim
    conv_w = conv_dw.shape[1]
    peer_heads = peer_subkeys.shape[1]
    tp, ts = bp * sp, bs * ss
    assert ts == SGU_CHUNK and sp % SGU_CHUNK == 0 and sp % CHUNK == 0 and window % CHUNK == 0

    n_c = bp + bs
    rows = -(-n_c // 8) * 8
    c_all = jnp.concatenate([c_prompt, c_sample, jnp.zeros((rows - n_c, d), F32)], axis=0)
    mods = ada_all(c_all, ada_w, ada_b)

    def mod_vectors(layer):
        m = mods[layer].reshape(rows, 6, d)
        prompt = [m[:bp, k][:, None, :] for k in range(6)]
        sample = [jnp.repeat(m[bp:n_c, k], ss, axis=0)[None] for k in range(6)]
        return prompt, sample

    t_idx = jnp.arange(SGU_CHUNK)
    mask_p = ((t_idx[None, :] // CHUNK) <= (t_idx[:, None] // CHUNK)).astype(F32)
    blk = t_idx // ss
    mask_s = ((blk[None, :] == blk[:, None])
              & ((t_idx[None, :] % ss) // CHUNK <= (t_idx[:, None] % ss) // CHUNK)).astype(F32)

    sgu_w_in_bf, sgu_w_out_bf = sgu_w_in.astype(BF16), sgu_w_out.astype(BF16)
    conv_w_in_bf, conv_w_out_bf = conv_w_in.astype(BF16), conv_w_out.astype(BF16)
    attn_w_qkv_bf, attn_w_o_bf = attn_w_qkv.astype(BF16), attn_w_o.astype(BF16)
    nk, half = peer_subkeys.shape[3:]
    zk = jnp.zeros((depth, peer_heads, nk, half), F32)
    skbd = jnp.concatenate([jnp.concatenate([peer_subkeys[:, :, 0], zk], axis=3),
                            jnp.concatenate([zk, peer_subkeys[:, :, 1]], axis=3)], axis=2).astype(BF16)
    pw = dict(wqt=jnp.swapaxes(peer_w_q, 1, 2).astype(BF16), skbd=skbd,
              u=peer_u.astype(BF16), vt=jnp.swapaxes(peer_v, 1, 2).astype(BF16))

    xp = x_prompt.reshape(tp, d)
    xs = x_sample.reshape(ts, d)
    pend_p = pend_s = None
    ia = ib = ic = 0
    conv_p, conv_s, kwin_p, vwin_p, kwin_s, vwin_s, sgu_s = [], [], [], [], [], [], []
    for layer in range(depth):
        mp, ms = mod_vectors(layer)
        hp = norm(xp, norm_mix_g[layer], mp[1], mp[0], seq_len=sp, pending=pend_p, emit_x=pend_p is not None)
        hs = norm(xs, norm_mix_g[layer], ms[1], ms[0], seq_len=ts, pending=pend_s, emit_x=pend_s is not None)
        if pend_p is not None:
            (hp, xp), (hs, xs) = hp, hs
        kind = layer % N_MIXERS
        if kind == 0:
            w = dict(w_in=(sgu_w_in_bf, ia), b_in=sgu_b_in[ia], ln_g=sgu_ln_g[ia],
                     ln_b=sgu_ln_b[ia], w_out=(sgu_w_out_bf, ia))
            ws = sgu_w_s[ia]
            wp = dict(w, ws=ws, mask=mask_p, bst=sgu_b_s[ia].T)
            reps = SGU_CHUNK // ss
            wsm = dict(w, ws=jnp.tile(ws[:, :ss, :ss], (1, reps, reps)), mask=mask_s,
                       bst=jnp.tile(sgu_b_s[ia][:, :ss], (1, reps)).T)
            xp, _ = _sgu_layer(xp, hp, mp[2], wp, seq_len=sp, emit_v=False)
            xs, v_rows = _sgu_layer(xs, hs, ms[2], wsm, seq_len=ts, emit_v=True)
            sgu_s.append(v_rows.reshape(bs, ss, -1))
            ia += 1
        elif kind == 1:
            dw = conv_dw[ib]
            w = dict(w_in=(conv_w_in_bf, ib), b_in=conv_b_in[ib], dw=dw, dw_b=conv_dw_b[ib],
                     ln_g=conv_ln_g[ib], ln_b=conv_ln_b[ib], w_out=(conv_w_out_bf, ib),
                     b_out=conv_b_out[ib])
            pad = HALO - (conv_w - 1)
            hist_p = jnp.zeros((bp, HALO, d), F32)
            hist_s = jnp.pad(state_conv[ib], ((0, 0), (pad, 0), (0, 0)))
            xp, glu_p = _conv_layer(xp, hp, mp[2], w, hist_p, seq_len=sp, mod_seq=sp)
            xs, glu_s = _conv_layer(xs, hs, ms[2], w, hist_s, seq_len=ss, mod_seq=ts)
            conv_p.append(glu_p.reshape(bp, sp, d)[:, sp - (conv_w - 1):])
            conv_s.append(jnp.concatenate([state_conv[ib], glu_s.reshape(bs, ss, d)], axis=1)[:, -(conv_w - 1):])
            ib += 1
        else:
            w = dict(w_qkv=(attn_w_qkv_bf, ic), b_qkv=attn_b_qkv[ic], sinks=attn_sinks[ic],
                     w_o=(attn_w_o_bf, ic))
            caches = (cache_k_win[ic].reshape(bs, window, kv_dim), cache_v_win[ic].reshape(bs, window, kv_dim))
            common = dict(q_dim=q_dim, head_dim=head_dim, window=window)
            xp, kp, vp = _attn_layer(xp, hp, mp[2], w, None, batch=bp, seq_len=sp, mod_seq=sp, **common)
            xs, ks, vs = _attn_layer(xs, hs, ms[2], w, caches, batch=bs, seq_len=ss, mod_seq=ts, **common)
            kwin_p.append(kp.reshape(bp, window, kv_heads, head_dim))
            vwin_p.append(vp.reshape(bp, window, kv_heads, head_dim))
            kwin_s.append(ks.reshape(bs, window, kv_heads, head_dim))
            vwin_s.append(vs.reshape(bs, window, kv_heads, head_dim))
            ic += 1

        pend_p = (_peer_layer(xp, norm_ch_g[layer], mp[4], mp[3], pw, layer, seq_len=sp), mp[5])
        pend_s = (_peer_layer(xs, norm_ch_g[layer], ms[4], ms[3], pw, layer, seq_len=ts), ms[5])

    y_prompt = norm(xp, norm_final_g, seq_len=sp, out_dtype=F32, pending=pend_p).reshape(bp, sp, d)
    y_sample = norm(xs, norm_final_g, seq_len=ts, out_dtype=F32, pending=pend_s).reshape(bs, ss, d)
    return (y_prompt, y_sample, jnp.stack(conv_p), jnp.stack(kwin_p), jnp.stack(vwin_p),
            jnp.stack(conv_s), jnp.stack(kwin_s), jnp.stack(vwin_s), jnp.stack(sgu_s))
```
